```python
import jax, jax.numpy as jnp
from jax import lax
import numpy as np

D_MODEL = 1024
BATCH = 8
SEQ = 2048
DEPTH = 2
DEC_BATCH = 128
DEC_SEQ = 1
PAST_LEN = 16384
PAGE_SIZE = 128

N_META = 16
N_MIXERS = 2
N_RET_LAYERS = (DEPTH + 1) // 2
N_CONV_LAYERS = DEPTH // 2
RET_HEADS = 4
RET_DK = D_MODEL // RET_HEADS
RET_DV = 2 * RET_DK
RET_VDIM = RET_HEADS * RET_DV
RET_IN = 2 * D_MODEL + 2 * RET_VDIM
RET_CHUNK = 128
ROPE_BASE = 10000.0
CONV_DIM = D_MODEL
CONV_WIDTH = 3
D_FF = 2816
FFN_CONV_WIDTH = 3
NORM_EPS = 1e-6
GN_EPS = 1e-6

kernel_name = "retnet_shortconv_convffn_decoder_step"


def rms_norm(x, g):
    xf = x.astype(jnp.float32)
    y = xf * lax.rsqrt(jnp.mean(xf * xf, axis=-1, keepdims=True) + NORM_EPS)
    return (y * g.astype(jnp.float32)).astype(x.dtype)


def rotary(x, pos):
    d = x.shape[-1]
    inv = 1.0 / (ROPE_BASE ** jnp.linspace(0.0, 1.0, d // 2, dtype=jnp.float32))
    ang = pos.astype(jnp.float32)[:, None] * inv[None, :]
    cos = jnp.cos(ang)[None, :, None, :]
    sin = jnp.sin(ang)[None, :, None, :]
    xf = x.astype(jnp.float32).reshape(x.shape[:-1] + (d // 2, 2))
    x1, x2 = xf[..., 0], xf[..., 1]
    out = jnp.stack([x1 * cos - x2 * sin, x1 * sin + x2 * cos], axis=-1)
    return out.reshape(x.shape).astype(x.dtype)


def retention_log_decay():
    return jnp.log(1.0 - 2.0 ** (-5.0 - jnp.arange(RET_HEADS, dtype=jnp.float32)))


def retention_chunk(q, k, v, s_prev, log_gamma):
    L = q.shape[2]
    n = jnp.arange(L, dtype=jnp.float32)
    diff = n[:, None] - n[None, :]
    causal = diff >= 0
    lg = log_gamma[:, None, None]
    decay_mask = jnp.where(causal[None], jnp.exp(lg * jnp.where(causal, diff, 0.0)[None]), 0.0)
    scores = jnp.einsum('bhnd,bhmd->bhnm', q, k) * decay_mask[None].astype(q.dtype)
    inner = jnp.einsum('bhnm,bhme->bhne', scores, v)
    cross_decay = jnp.exp(log_gamma[:, None] * (n[None, :] + 1.0))
    cross = jnp.einsum('bhnd,bhde->bhne', q, s_prev.astype(q.dtype)) * cross_decay[None, :, :, None].astype(q.dtype)
    k_decay = jnp.exp(log_gamma[:, None] * (L - 1.0 - n[None, :]))
    kv = jnp.einsum('bhmd,bhme->bhde', k * k_decay[None, :, :, None].astype(k.dtype), v)
    s_new = jnp.exp(log_gamma * L)[None, :, None, None] * s_prev + kv
    return inner + cross, s_new.astype(s_prev.dtype)


def retention_sequence(q, k, v, s0, lead):
    lg = retention_log_decay()
    B, H, T, _ = q.shape
    o_lead, s = retention_chunk(q[:, :, :lead], k[:, :, :lead], v[:, :, :lead], s0, lg)
    nc = (T - lead) // RET_CHUNK
    if nc == 0:
        return o_lead, s

    def blocks(t):
        return t[:, :, lead:].reshape(B, H, nc, RET_CHUNK, t.shape[-1]).transpose(2, 0, 1, 3, 4)

    def step(carry, blk):
        qc, kc, vc = blk
        o, carry = retention_chunk(qc, kc, vc, carry, lg)
        return carry, o

    s, o_rest = lax.scan(step, s, (blocks(q), blocks(k), blocks(v)))
    o_rest = o_rest.transpose(1, 2, 0, 3, 4).reshape(B, H, nc * RET_CHUNK, RET_DV)
    return jnp.concatenate([o_lead, o_rest], axis=2), s


def retention_mixer(h, pos, s0, w_in, w_out, lead):
    B, T, _ = h.shape
    proj = h @ w_in
    q, k, v, g = jnp.split(proj, [D_MODEL, 2 * D_MODEL, 2 * D_MODEL + RET_VDIM], axis=-1)
    q = rotary(q.reshape(B, T, RET_HEADS, RET_DK), pos)
    k = rotary(k.reshape(B, T, RET_HEADS, RET_DK), pos) * (RET_DK ** -0.5)
    v = v.reshape(B, T, RET_HEADS, RET_DV)
    q, k, v = (t.transpose(0, 2, 1, 3) for t in (q, k, v))
    o, s = retention_sequence(q, k, v, s0, lead)
    of = o.astype(jnp.float32)
    mu = jnp.mean(of, axis=-1, keepdims=True)
    var = jnp.mean(jnp.square(of - mu), axis=-1, keepdims=True)
    o = ((of - mu) * lax.rsqrt(var + GN_EPS)).astype(h.dtype)
    o = o.transpose(0, 2, 1, 3).reshape(B, T, RET_VDIM)
    return (jax.nn.silu(g) * o) @ w_out, s


def dwconv_valid(ext, w):
    W = w.shape[0]
    T = ext.shape[1] - W + 1
    acc = ext[:, 0:T] * w[0]
    for j in range(1, W):
        acc = acc + ext[:, j:j + T] * w[j]
    return acc


def shortconv_mixer(h, conv_state, w_in, conv_w, w_out):
    bg, cg, xin = jnp.split(h @ w_in, 3, axis=-1)
    u = cg * xin
    ext = jnp.concatenate([conv_state.astype(u.dtype), u], axis=1)
    y = dwconv_valid(ext, conv_w)
    return (bg * y) @ w_out, ext[:, -(CONV_WIDTH - 1):]


def conv_ffn(h, ffn_state, w_in, conv_w, conv_b, w_down):
    up, gate = jnp.split(h @ w_in, 2, axis=-1)
    ext = jnp.concatenate([ffn_state.astype(up.dtype), up], axis=1)
    a = dwconv_valid(ext, conv_w) + conv_b
    return (jax.nn.silu(a) * gate) @ w_down, ext[:, -(FFN_CONV_WIDTH - 1):]


def trunk(x, pos, lead, ret_states, conv_states, ffn_states, norm_mix, norm_ffn, norm_final,
          w_ret_in, w_ret_out, w_sc_in, w_sc_conv, w_sc_out, w_ffn_in, w_ffn_conv, b_ffn_conv, w_ffn_out):
    new_ret, new_conv, new_ffn = [], [], []
    for i in range(DEPTH):
        h = rms_norm(x, norm_mix[i])
        if i % N_MIXERS == 0:
            r = i // N_MIXERS
            m, s = retention_mixer(h, pos, ret_states[r], w_ret_in[r], w_ret_out[r], lead)
            new_ret.append(s)
        else:
            c = i // N_MIXERS
            m, s = shortconv_mixer(h, conv_states[c], w_sc_in[c], w_sc_conv[c], w_sc_out[c])
            new_conv.append(s)
        x = x + m
        h = rms_norm(x, norm_ffn[i])
        f, s = conv_ffn(h, ffn_states[i], w_ffn_in[i], w_ffn_conv[i], b_ffn_conv[i], w_ffn_out[i])
        new_ffn.append(s)
        x = x + f
    return rms_norm(x, norm_final), jnp.stack(new_ret), jnp.stack(new_conv), jnp.stack(new_ffn)


def setup_inputs(seed: int = 0) -> dict:
    key = jax.random.key(seed)
    ks = jax.random.split(key, 20)
    nrm = jax.random.normal
    f32 = jnp.float32
    return {
        "x_prompt": nrm(ks[0], (BATCH, SEQ, D_MODEL), f32),
        "x_sample": nrm(ks[1], (DEC_BATCH, DEC_SEQ, D_MODEL), f32),
        "state_ret": 0.05 * nrm(ks[2], (N_RET_LAYERS, DEC_BATCH, RET_HEADS, RET_DK, RET_DV), f32),
        "state_conv": nrm(ks[3], (N_CONV_LAYERS, DEC_BATCH, CONV_WIDTH - 1, CONV_DIM), f32),
        "state_ffn": nrm(ks[4], (DEPTH, DEC_BATCH, FFN_CONV_WIDTH - 1, D_FF), f32),
        "meta_tokens": nrm(ks[5], (N_META, D_MODEL), f32),
        "norm_mix": 1.0 + 0.05 * nrm(ks[6], (DEPTH, D_MODEL), f32),
        "norm_ffn": 1.0 + 0.05 * nrm(ks[7], (DEPTH, D_MODEL), f32),
        "norm_final": 1.0 + 0.05 * nrm(ks[8], (D_MODEL,), f32),
        "w_ret_in": nrm(ks[9], (N_RET_LAYERS, D_MODEL, RET_IN), f32) * D_MODEL ** -0.5,
        "w_ret_out": nrm(ks[10], (N_RET_LAYERS, RET_VDIM, D_MODEL), f32) * RET_VDIM ** -0.5,
        "w_sc_in": nrm(ks[11], (N_CONV_LAYERS, D_MODEL, 3 * CONV_DIM), f32) * D_MODEL ** -0.5,
        "w_sc_conv": nrm(ks[12], (N_CONV_LAYERS, CONV_WIDTH, CONV_DIM), f32) * CONV_WIDTH ** -0.5,
        "w_sc_out": nrm(ks[13], (N_CONV_LAYERS, CONV_DIM, D_MODEL), f32) * CONV_DIM ** -0.5,
        "w_ffn_in": nrm(ks[14], (DEPTH, D_MODEL, 2 * D_FF), f32) * D_MODEL ** -0.5,
        "w_ffn_conv": nrm(ks[15], (DEPTH, FFN_CONV_WIDTH, D_FF), f32) * FFN_CONV_WIDTH ** -0.5,
        "b_ffn_conv": 0.02 * nrm(ks[16], (DEPTH, D_FF), f32),
        "w_ffn_out": nrm(ks[17], (DEPTH, D_FF, D_MODEL), f32) * D_FF ** -0.5,
    }


def reference(x_prompt, x_sample, state_ret, state_conv, state_ffn, meta_tokens, norm_mix, norm_ffn,
              norm_final, w_ret_in, w_ret_out, w_sc_in, w_sc_conv, w_sc_out, w_ffn_in, w_ffn_conv,
              b_ffn_conv, w_ffn_out):
    weights = (norm_mix, norm_ffn, norm_final, w_ret_in, w_ret_out, w_sc_in, w_sc_conv, w_sc_out,
               w_ffn_in, w_ffn_conv, b_ffn_conv, w_ffn_out)
    B, S, _ = x_prompt.shape
    dt = x_prompt.dtype
    meta = jnp.broadcast_to(meta_tokens.astype(dt)[None], (B, N_META, D_MODEL))
    xp = jnp.concatenate([meta, x_prompt], axis=1)
    pos_p = jnp.arange(N_META + S, dtype=jnp.int32)
    zr = jnp.zeros((N_RET_LAYERS, B, RET_HEADS, RET_DK, RET_DV), dt)
    zc = jnp.zeros((N_CONV_LAYERS, B, CONV_WIDTH - 1, CONV_DIM), dt)
    zf = jnp.zeros((DEPTH, B, FFN_CONV_WIDTH - 1, D_FF), dt)
    yp, ret_p, conv_p, ffn_p = trunk(xp, pos_p, N_META, zr, zc, zf, *weights)
    y_prompt = yp[:, N_META:]
    Ts = x_sample.shape[1]
    pos_s = PAST_LEN + jnp.arange(Ts, dtype=jnp.int32)
    y_sample, ret_s, conv_s, ffn_s = trunk(x_sample, pos_s, Ts, state_ret, state_conv, state_ffn, *weights)
    return (y_prompt, y_sample, ret_p, ret_s, conv_p, conv_s, ffn_p, ffn_s)
```

```python
import functools
import math

import jax
import jax.numpy as jnp
from jax import lax
from jax.experimental import pallas as pl
from jax.experimental.pallas import tpu as pltpu

F32 = jnp.float32
BF16 = jnp.bfloat16

D_MODEL = 1024
SEQ = 2048
BATCH = 8
DEC_BATCH = 128
PAST_LEN = 16384
N_META = 16
RET_HEADS = 4
RET_DK = 256
RET_DV = 512
RET_VDIM = RET_HEADS * RET_DV
RET_IN = 2 * D_MODEL + 2 * RET_VDIM
ROPE_BASE = 10000.0
D_FF = 2816
NORM_EPS = 1e-6
GN_EPS = 1e-6

LANES = 128
SMALL_BLOCK = 128
SMALL_ROWS = 2 * SMALL_BLOCK
VMEM_LIMIT = 56 * 1024 * 1024


def _params(n_axes):
    return pltpu.CompilerParams(
        dimension_semantics=("arbitrary",) * n_axes,
        vmem_limit_bytes=VMEM_LIMIT)


def _const_spec(shape):
    zeros = (0,) * len(shape)
    return pl.BlockSpec(shape, lambda *_: zeros, pipeline_mode=pl.Buffered(1))


def _rms(x, g):
    ms = jnp.mean(x * x, axis=-1, keepdims=True)
    return x * lax.rsqrt(ms + NORM_EPS) * g


def _ret_inproj_kernel(x_ref, g_ref, w_ref, c_ref, sa_ref, sb_ref, o_ref, *, cw):
    h = _rms(x_ref[...], g_ref[...]).astype(BF16)
    for c0 in range(0, RET_IN, cw):
        z = jnp.dot(h, w_ref[:, c0:c0 + cw], preferred_element_type=F32)
        if c0 >= 2 * D_MODEL:
            o_ref[:, c0:c0 + cw] = z.astype(o_ref.dtype)
            continue
        for s0 in range(0, cw, LANES):
            col = (c0 + s0) % RET_DK
            zs = z[:, s0:s0 + LANES]
            r = (zs * c_ref[:, col:col + LANES]
                 + pltpu.roll(zs, LANES - 1, 1) * sa_ref[:, col:col + LANES]
                 + pltpu.roll(zs, 1, 1) * sb_ref[:, col:col + LANES])
            if c0 >= D_MODEL:
                r = r * (RET_DK ** -0.5)
            o_ref[:, c0 + s0:c0 + s0 + LANES] = r.astype(o_ref.dtype)


def _ret_inproj(x, g, w, tabs, *, tm, tab_blocks, out_dtype):
    rows = x.shape[0]
    cos, sa, sb = tabs
    tab_spec = pl.BlockSpec((tm, RET_DK), lambda i: (i % tab_blocks, 0))
    return pl.pallas_call(
        functools.partial(_ret_inproj_kernel, cw=512),
        grid=(rows // tm,),
        in_specs=[
            pl.BlockSpec((tm, D_MODEL), lambda i: (i, 0)),
            _const_spec((1, D_MODEL)),
            _const_spec((D_MODEL, RET_IN)),
            tab_spec, tab_spec, tab_spec,
        ],
        out_specs=pl.BlockSpec((tm, RET_IN), lambda i: (i, 0)),
        out_shape=jax.ShapeDtypeStruct((rows, RET_IN), out_dtype),
        compiler_params=_params(1),
        name="ret_inproj",
    )(x, g, w, cos, sa, sb)


def _gn_gate(o, g):
    mu = jnp.mean(o, axis=-1, keepdims=True)
    d = o - mu
    var = jnp.mean(d * d, axis=-1, keepdims=True)
    on = d * lax.rsqrt(var + GN_EPS)
    gf = g.astype(F32)
    return (gf / (1.0 + jnp.exp(-gf))) * on


def _ret_core_kernel(p_ref, x_ref, s0_ref, wo_ref, xo_ref, sf_ref, s_scr, gated_scr,
                     *, chunk, valid):
    c = pl.program_id(1)

    @pl.when(c == 0)
    def _():
        s_scr[...] = s0_ref[...]

    n_col = lax.broadcasted_iota(jnp.int32, (chunk, 1), 0).astype(F32)
    n_mat = lax.broadcasted_iota(jnp.int32, (chunk, chunk), 0)
    m_mat = lax.broadcasted_iota(jnp.int32, (chunk, chunk), 1)
    diff = (n_mat - m_mat).astype(F32)
    causal = diff >= 0
    diff = jnp.where(causal, diff, 0.0)

    for h in range(RET_HEADS):
        lg = math.log(1.0 - 2.0 ** (-5.0 - h))
        q = p_ref[:, h * RET_DK:(h + 1) * RET_DK].astype(BF16)
        k = p_ref[:, D_MODEL + h * RET_DK:D_MODEL + (h + 1) * RET_DK]
        v = p_ref[:, 2 * D_MODEL + h * RET_DV:2 * D_MODEL + (h + 1) * RET_DV].astype(BF16)
        g = p_ref[:, 2 * D_MODEL + RET_VDIM + h * RET_DV:
                  2 * D_MODEL + RET_VDIM + (h + 1) * RET_DV]
        s_prev = s_scr[h]

        scores = lax.dot_general(q, k.astype(BF16), (((1,), (1,)), ((), ())),
                                 preferred_element_type=F32)
        decay = jnp.where(causal, jnp.exp(lg * diff), 0.0)
        inner = jnp.dot((scores * decay).astype(BF16), v, preferred_element_type=F32)
        cross = jnp.dot(q, s_prev.astype(BF16), preferred_element_type=F32)
        o = inner + cross * jnp.exp(lg * (n_col + 1.0))

        k_decay = jnp.where(n_col < valid, jnp.exp(lg * (valid - 1.0 - n_col)), 0.0)
        kd = (k.astype(F32) * k_decay).astype(BF16)
        kv = lax.dot_general(kd, v, (((0,), (0,)), ((), ())),
                             preferred_element_type=F32)
        s_scr[h] = math.exp(lg * valid) * s_prev + kv

        gated_scr[:, h * RET_DV:(h + 1) * RET_DV] = _gn_gate(o, g).astype(BF16)

    xo_ref[...] = x_ref[...] + jnp.dot(gated_scr[...], wo_ref[...],
                                       preferred_element_type=F32)

    @pl.when(c == pl.num_programs(1) - 1)
    def _():
        sf_ref[0] = s_scr[...]


def _ret_core(proj, x, s0, w_out, *, n_seq, n_chunks, chunk, valid):
    rows = n_seq * n_chunks * chunk
    row_map = lambda b, c: (b * n_chunks + c, 0)
    state_shape = (RET_HEADS, RET_DK, RET_DV)
    return pl.pallas_call(
        functools.partial(_ret_core_kernel, chunk=chunk, valid=valid),
        grid=(n_seq, n_chunks),
        in_specs=[
            pl.BlockSpec((chunk, RET_IN), row_map),
            pl.BlockSpec((chunk, D_MODEL), row_map),
            _const_spec(state_shape),
            _const_spec((RET_VDIM, D_MODEL)),
        ],
        out_specs=[
            pl.BlockSpec((chunk, D_MODEL), row_map),
            pl.BlockSpec((1,) + state_shape, lambda b, c: (b, 0, 0, 0)),
        ],
        out_shape=[
            jax.ShapeDtypeStruct((rows, D_MODEL), F32),
            jax.ShapeDtypeStruct((n_seq,) + state_shape, F32),
        ],
        scratch_shapes=[
            pltpu.VMEM(state_shape, F32),
            pltpu.VMEM((chunk, RET_VDIM), BF16),
        ],
        compiler_params=_params(2),
        name="ret_core",
    )(proj, x, s0, w_out)


def _ret_sample_kernel(p_ref, s_ref, o_ref, sn_ref, *, bb):
    i = pl.program_id(0)
    for j in range(bb):
        b = i * bb + j
        row = p_ref[pl.ds(b, 1), :]
        for h in range(RET_HEADS):
            gamma = 1.0 - 2.0 ** (-5.0 - h)
            q = row[:, h * RET_DK:(h + 1) * RET_DK]
            k = row[:, D_MODEL + h * RET_DK:D_MODEL + (h + 1) * RET_DK]
            v = row[:, 2 * D_MODEL + h * RET_DV:2 * D_MODEL + (h + 1) * RET_DV]
            s_prev = s_ref[j, h]
            q8 = jnp.broadcast_to(q, (8, RET_DK)).astype(BF16)
            cross = jnp.dot(q8, s_prev.astype(BF16), preferred_element_type=F32)[0:1]
            qk = jnp.sum(q * k, axis=-1, keepdims=True)
            o_ref[pl.ds(b, 1), h * RET_DV:(h + 1) * RET_DV] = qk * v + gamma * cross
            k8 = jnp.broadcast_to(k, (8, RET_DK)).astype(BF16)
            v8 = jnp.broadcast_to(v * 0.125, (8, RET_DV)).astype(BF16)
            kv = lax.dot_general(k8, v8, (((0,), (0,)), ((), ())),
                                 preferred_element_type=F32)
            sn_ref[j, h] = gamma * s_prev + kv


def _ret_sample(proj_small, state, *, bb=2):
    state_block = (bb, RET_HEADS, RET_DK, RET_DV)
    return pl.pallas_call(
        functools.partial(_ret_sample_kernel, bb=bb),
        grid=(DEC_BATCH // bb,),
        in_specs=[
            pl.BlockSpec((SMALL_BLOCK, RET_IN), lambda i: (1, 0),
                         pipeline_mode=pl.Buffered(1)),
            pl.BlockSpec(state_block, lambda i: (i, 0, 0, 0)),
        ],
        out_specs=[
            pl.BlockSpec((DEC_BATCH, RET_VDIM), lambda i: (0, 0)),
            pl.BlockSpec(state_block, lambda i: (i, 0, 0, 0)),
        ],
        out_shape=[
            jax.ShapeDtypeStruct((DEC_BATCH, RET_VDIM), F32),
            jax.ShapeDtypeStruct(state.shape, F32),
        ],
        compiler_params=_params(1),
        name="ret_sample",
    )(proj_small, state)


def _gn_outproj_kernel(o_ref, g_ref, x_ref, wo_ref, xo_ref):
    parts = []
    for h in range(RET_HEADS):
        sl = slice(h * RET_DV, (h + 1) * RET_DV)
        parts.append(_gn_gate(o_ref[:, sl], g_ref[:, sl]).astype(BF16))
    gated = jnp.concatenate(parts, axis=1)
    xo_ref[...] = x_ref[...] + jnp.dot(gated, wo_ref[...], preferred_element_type=F32)


def _gn_outproj(o, proj_small, x_small, w_out):
    return pl.pallas_call(
        _gn_outproj_kernel,
        grid=(1,),
        in_specs=[
            _const_spec((DEC_BATCH, RET_VDIM)),
            pl.BlockSpec((SMALL_BLOCK, RET_VDIM), lambda i: (1, 2),
                         pipeline_mode=pl.Buffered(1)),
            pl.BlockSpec((SMALL_BLOCK, D_MODEL), lambda i: (1, 0),
                         pipeline_mode=pl.Buffered(1)),
            _const_spec((RET_VDIM, D_MODEL)),
        ],
        out_specs=pl.BlockSpec((DEC_BATCH, D_MODEL), lambda i: (0, 0)),
        out_shape=jax.ShapeDtypeStruct((DEC_BATCH, D_MODEL), F32),
        compiler_params=_params(1),
        name="gn_outproj",
    )(o, proj_small, x_small, w_out)


def _mlp_pre(h, w_ref, c0, cw, dff, kind):
    dot = lambda off: jnp.dot(h, w_ref[:, off + c0:off + c0 + cw],
                              preferred_element_type=F32)
    if kind == "ffn":
        return dot(0), dot(dff)
    bg = dot(0)
    return dot(dff) * dot(2 * dff), bg


def _mlp_act(pre, p1, p2, other, cw_ref, cb_ref, c0, cw, kind):
    sl = slice(c0, c0 + cw)
    conv = cw_ref[0:1, sl] * p2 + cw_ref[1:2, sl] * p1 + cw_ref[2:3, sl] * pre
    if kind == "ffn":
        a = conv + cb_ref[:, sl]
        return (a / (1.0 + jnp.exp(-a))) * other
    return other * conv


def _mlp_seq_kernel(x_ref, g_ref, w_ref, cw_ref, cb_ref, wo_ref, carry0_ref, gf_ref,
                    xo_ref, st_ref, carry_scr, pre_scr, act_scr,
                    *, tm, dff, cw, kind, final_norm):
    @pl.when(pl.program_id(1) == 0)
    def _():
        carry_scr[...] = carry0_ref[...]

    h = _rms(x_ref[...], g_ref[...]).astype(BF16)
    for c0 in range(0, dff, cw):
        sl = slice(c0, c0 + cw)
        pre, other = _mlp_pre(h, w_ref, c0, cw, dff, kind)
        pre_scr[0:8, :] = carry_scr[:, sl]
        pre_scr[8:8 + tm, :] = pre
        p1 = pre_scr[7:7 + tm, :]
        p2 = pre_scr[6:6 + tm, :]
        act_scr[:, sl] = _mlp_act(pre, p1, p2, other, cw_ref, cb_ref, c0, cw, kind).astype(BF16)
        carry_scr[:, sl] = pre[tm - 8:tm, :]
    out = x_ref[...] + jnp.dot(act_scr[...], wo_ref[...], preferred_element_type=F32)
    if final_norm:
        out = _rms(out, gf_ref[...])
    xo_ref[...] = out
    st_ref[0] = carry_scr[6:8, :]


def _mlp_seq(x, g, w_in, conv_w, conv_b, w_out, carry0, g_final, *, kind, final_norm, tm=512):
    dff = w_out.shape[0]
    tiles = SEQ // tm
    row_map = lambda b, t: (b * tiles + t, 0)
    return pl.pallas_call(
        functools.partial(_mlp_seq_kernel, tm=tm, dff=dff, cw=256, kind=kind,
                          final_norm=final_norm),
        grid=(BATCH, tiles),
        in_specs=[
            pl.BlockSpec((tm, D_MODEL), row_map),
            _const_spec((1, D_MODEL)),
            _const_spec(w_in.shape),
            _const_spec((3, dff)),
            _const_spec((1, dff)),
            _const_spec((dff, D_MODEL)),
            _const_spec((8, dff)),
            _const_spec((1, D_MODEL)),
        ],
        out_specs=[
            pl.BlockSpec((tm, D_MODEL), row_map),
            pl.BlockSpec((1, 2, dff), lambda b, t: (b, 0, 0)),
        ],
        out_shape=[
            jax.ShapeDtypeStruct(x.shape, F32),
            jax.ShapeDtypeStruct((BATCH, 2, dff), F32),
        ],
        scratch_shapes=[
            pltpu.VMEM((8, dff), F32),
            pltpu.VMEM((tm + 8, 256), F32),
            pltpu.VMEM((tm, dff), BF16),
        ],
        compiler_params=_params(2),
        name="mlp_seq_" + kind,
    )(x, g, w_in, conv_w, conv_b, w_out, carry0, g_final)


def _mlp_small_kernel(x_ref, g_ref, w_ref, cw_ref, cb_ref, wo_ref, st_ref, gf_ref,
                      xo_ref, pre_ref, pre_scr, act_scr, *, dff, cw, kind, final_norm):
    nb = SMALL_BLOCK
    h = _rms(x_ref[...], g_ref[...]).astype(BF16)
    for c0 in range(0, dff, cw):
        sl = slice(c0, c0 + cw)
        pre, other = _mlp_pre(h, w_ref, c0, cw, dff, kind)
        pre_ref[:, sl] = pre
        pre_scr[0:8, :] = jnp.zeros((8, cw), F32)
        pre_scr[8:8 + nb, :] = pre[0:nb]
        act_scr[0:nb, sl] = _mlp_act(pre[0:nb], pre_scr[7:7 + nb, :], pre_scr[6:6 + nb, :],
                                     other[0:nb], cw_ref, cb_ref, c0, cw, kind).astype(BF16)
        act_scr[nb:2 * nb, sl] = _mlp_act(pre[nb:2 * nb], st_ref[:, dff + c0:dff + c0 + cw],
                                          st_ref[:, sl], other[nb:2 * nb],
                                          cw_ref, cb_ref, c0, cw, kind).astype(BF16)
    out = x_ref[...] + jnp.dot(act_scr[...], wo_ref[...], preferred_element_type=F32)
    if final_norm:
        out = _rms(out, gf_ref[...])
    xo_ref[...] = out


def _mlp_small(x, g, w_in, conv_w, conv_b, w_out, state2, g_final, *, kind, final_norm):
    dff = w_out.shape[0]
    return pl.pallas_call(
        functools.partial(_mlp_small_kernel, dff=dff, cw=256, kind=kind,
                          final_norm=final_norm),
        grid=(1,),
        in_specs=[
            _const_spec((SMALL_ROWS, D_MODEL)),
            _const_spec((1, D_MODEL)),
            _const_spec(w_in.shape),
            _const_spec((3, dff)),
            _const_spec((1, dff)),
            _const_spec((dff, D_MODEL)),
            _const_spec((DEC_BATCH, 2 * dff)),
            _const_spec((1, D_MODEL)),
        ],
        out_specs=[
            pl.BlockSpec((SMALL_ROWS, D_MODEL), lambda i: (0, 0)),
            pl.BlockSpec((SMALL_ROWS, dff), lambda i: (0, 0)),
        ],
        out_shape=[
            jax.ShapeDtypeStruct((SMALL_ROWS, D_MODEL), F32),
            jax.ShapeDtypeStruct((SMALL_ROWS, dff), F32),
        ],
        scratch_shapes=[
            pltpu.VMEM((SMALL_BLOCK + 8, 256), F32),
            pltpu.VMEM((SMALL_ROWS, dff), BF16),
        ],
        compiler_params=_params(1),
        name="mlp_small_" + kind,
    )(x, g, w_in, conv_w, conv_b, w_out, state2, g_final)


def _rope_tables(pos):
    half = RET_DK // 2
    inv = 1.0 / (ROPE_BASE ** jnp.linspace(0.0, 1.0, half, dtype=F32))
    ang = pos.astype(F32)[:, None] * inv[None, :]
    cos = jnp.repeat(jnp.cos(ang), 2, axis=1)
    sin = jnp.repeat(jnp.sin(ang), 2, axis=1)
    even = (jnp.arange(RET_DK) % 2 == 0)[None, :]
    sa = jnp.where(even, -sin, 0.0)
    sb = jnp.where(even, 0.0, sin)
    return cos, sa, sb


def kernel(x_prompt, x_sample, state_ret, state_conv, state_ffn, meta_tokens, norm_mix, norm_ffn,
           norm_final, w_ret_in, w_ret_out, w_sc_in, w_sc_conv, w_sc_out, w_ffn_in, w_ffn_conv,
           b_ffn_conv, w_ffn_out):
    nb = SMALL_BLOCK
    xb = x_prompt.reshape(BATCH * SEQ, D_MODEL)
    xs = jnp.concatenate([meta_tokens, jnp.zeros((nb - N_META, D_MODEL), F32),
                          x_sample.reshape(DEC_BATCH, D_MODEL)], axis=0)

    w_ret_in_b = w_ret_in[0].astype(BF16)
    w_ret_out_b = w_ret_out[0].astype(BF16)
    w_sc_in_b = w_sc_in[0].astype(BF16)
    w_sc_out_b = w_sc_out[0].astype(BF16)
    w_ffn_in_b = w_ffn_in.astype(BF16)
    w_ffn_out_b = w_ffn_out.astype(BF16)
    g_final = norm_final.reshape(1, D_MODEL)

    pos_big = N_META + jnp.arange(SEQ, dtype=jnp.int32)
    pos_small = jnp.concatenate([jnp.arange(nb, dtype=jnp.int32),
                                 jnp.full((nb,), PAST_LEN, jnp.int32)])
    tabs_big = _rope_tables(pos_big)
    tabs_small = _rope_tables(pos_small)

    g0 = norm_mix[0].reshape(1, D_MODEL)
    proj_s = _ret_inproj(xs, g0, w_ret_in_b, tabs_small, tm=SMALL_ROWS, tab_blocks=1,
                         out_dtype=F32)
    zero_state = jnp.zeros((RET_HEADS, RET_DK, RET_DV), F32)
    xs_meta, s_meta = _ret_core(proj_s, xs, zero_state, w_ret_out_b,
                                n_seq=1, n_chunks=1, chunk=nb, valid=N_META)
    o_s, ret_s = _ret_sample(proj_s, state_ret[0])
    xs_samp = _gn_outproj(o_s, proj_s, xs, w_ret_out_b)
    xs = jnp.concatenate([xs_meta, xs_samp], axis=0)

    tm_proj = 512
    proj_b = _ret_inproj(xb, g0, w_ret_in_b, tabs_big, tm=tm_proj, tab_blocks=SEQ // tm_proj,
                         out_dtype=BF16)
    chunk = 256
    xb, ret_p = _ret_core(proj_b, xb, s_meta[0], w_ret_out_b,
                          n_seq=BATCH, n_chunks=SEQ // chunk, chunk=chunk, valid=chunk)

    def ffn(i, xs, xb, final_norm):
        gi = norm_ffn[i].reshape(1, D_MODEL)
        cb = b_ffn_conv[i].reshape(1, D_FF)
        xs, pre_s = _mlp_small(xs, gi, w_ffn_in_b[i], w_ffn_conv[i], cb, w_ffn_out_b[i],
                               state_ffn[i].reshape(DEC_BATCH, 2 * D_FF), g_final,
                               kind="ffn", final_norm=final_norm)
        xb, st_p = _mlp_seq(xb, gi, w_ffn_in_b[i], w_ffn_conv[i], cb, w_ffn_out_b[i],
                            pre_s[N_META - 8:N_META], g_final, kind="ffn", final_norm=final_norm)
        st_s = jnp.stack([state_ffn[i, :, 1], pre_s[nb:]], axis=1)
        return xs, xb, st_p, st_s

    xs, xb, ffn_p0, ffn_s0 = ffn(0, xs, xb, False)

    g1 = norm_mix[1].reshape(1, D_MODEL)
    zero_b = jnp.zeros((1, D_MODEL), F32)
    xs, pre_s = _mlp_small(xs, g1, w_sc_in_b, w_sc_conv[0], zero_b, w_sc_out_b,
                           state_conv[0].reshape(DEC_BATCH, 2 * D_MODEL), g_final,
                           kind="sc", final_norm=False)
    xb, conv_p = _mlp_seq(xb, g1, w_sc_in_b, w_sc_conv[0], zero_b, w_sc_out_b,
                          pre_s[N_META - 8:N_META], g_final, kind="sc", final_norm=False)
    conv_s = jnp.stack([state_conv[0, :, 1], pre_s[nb:]], axis=1)

    xs, xb, ffn_p1, ffn_s1 = ffn(1, xs, xb, True)

    y_prompt = xb.reshape(BATCH, SEQ, D_MODEL)
    y_sample = xs[nb:].reshape(DEC_BATCH, 1, D_MODEL)
    return (y_prompt, y_sample, ret_p[None], ret_s[None], conv_p[None], conv_s[None],
            jnp.stack([ffn_p0, ffn_p1]), jnp.stack([ffn_s0, ffn_s1]))
```

```python
import functools
import math

import jax
import jax.numpy as jnp
from jax import lax
from jax.experimental import pallas as pl
from jax.experimental.pallas import tpu as pltpu

F32 = jnp.float32
BF16 = jnp.bfloat16

D_MODEL = 1024
SEQ = 2048
BATCH = 8
DEC_BATCH = 128
PAST_LEN = 16384
N_META = 16
RET_HEADS = 4
RET_DK = 256
RET_DV = 512
RET_VDIM = RET_HEADS * RET_DV
RET_IN = 2 * D_MODEL + 2 * RET_VDIM
ROPE_BASE = 10000.0
D_FF = 2816
NORM_EPS = 1e-6
GN_EPS = 1e-6

LANES = 128
SMALL_BLOCK = 128
SMALL_ROWS = 2 * SMALL_BLOCK
VMEM_LIMIT = 56 * 1024 * 1024


def _params(n_axes):
    return pltpu.CompilerParams(
        dimension_semantics=("arbitrary",) * n_axes,
        vmem_limit_bytes=VMEM_LIMIT)


def _const_spec(shape):
    zeros = (0,) * len(shape)
    return pl.BlockSpec(shape, lambda *_: zeros, pipeline_mode=pl.Buffered(1))


def _rms(x, g):
    ms = jnp.mean(x * x, axis=-1, keepdims=True)
    return x * lax.rsqrt(ms + NORM_EPS) * g


def _ret_inproj_kernel(x_ref, g_ref, w_ref, c_ref, sa_ref, sb_ref, o_ref, *, cw):
    h = _rms(x_ref[...], g_ref[...]).astype(BF16)
    for c0 in range(0, RET_IN, cw):
        z = jnp.dot(h, w_ref[:, c0:c0 + cw], preferred_element_type=F32)
        if c0 >= 2 * D_MODEL:
            o_ref[:, c0:c0 + cw] = z.astype(o_ref.dtype)
            continue
        for s0 in range(0, cw, LANES):
            col = (c0 + s0) % RET_DK
            zs = z[:, s0:s0 + LANES]
            r = (zs * c_ref[:, col:col + LANES]
                 + pltpu.roll(zs, LANES - 1, 1) * sa_ref[:, col:col + LANES]
                 + pltpu.roll(zs, 1, 1) * sb_ref[:, col:col + LANES])
            if c0 >= D_MODEL:
                r = r * (RET_DK ** -0.5)
            o_ref[:, c0 + s0:c0 + s0 + LANES] = r.astype(o_ref.dtype)


def _ret_inproj(x, g, w, tabs, *, tm, tab_blocks, out_dtype):
    rows = x.shape[0]
    cos, sa, sb = tabs
    tab_spec = pl.BlockSpec((tm, RET_DK), lambda i: (i % tab_blocks, 0))
    return pl.pallas_call(
        functools.partial(_ret_inproj_kernel, cw=512),
        grid=(rows // tm,),
        in_specs=[
            pl.BlockSpec((tm, D_MODEL), lambda i: (i, 0)),
            _const_spec((1, D_MODEL)),
            _const_spec((D_MODEL, RET_IN)),
            tab_spec, tab_spec, tab_spec,
        ],
        out_specs=pl.BlockSpec((tm, RET_IN), lambda i: (i, 0)),
        out_shape=jax.ShapeDtypeStruct((rows, RET_IN), out_dtype),
        compiler_params=_params(1),
        name="ret_inproj",
    )(x, g, w, cos, sa, sb)


def _gn_gate(o, g):
    mu = jnp.mean(o, axis=-1, keepdims=True)
    d = o - mu
    var = jnp.mean(d * d, axis=-1, keepdims=True)
    on = d * lax.rsqrt(var + GN_EPS)
    gf = g.astype(F32)
    return (gf / (1.0 + jnp.exp(-gf))) * on


def _ret_core_kernel(p_ref, x_ref, s0_ref, wo_ref, xo_ref, sf_ref,
                     s_scr, gated_scr, decay_scr, cdec_scr, kdec_scr, *, chunk, valid):
    c = pl.program_id(1)

    @pl.when((pl.program_id(0) == 0) & (c == 0))
    def _():
        n_mat = lax.broadcasted_iota(jnp.int32, (chunk, chunk), 0)
        m_mat = lax.broadcasted_iota(jnp.int32, (chunk, chunk), 1)
        diff = (n_mat - m_mat).astype(F32)
        causal = diff >= 0
        diff = jnp.where(causal, diff, 0.0)
        n_col = lax.broadcasted_iota(jnp.int32, (chunk, LANES), 0).astype(F32)
        for h in range(RET_HEADS):
            lg = math.log(1.0 - 2.0 ** (-5.0 - h))
            decay_scr[h] = jnp.where(causal, jnp.exp(lg * diff), 0.0)
            cdec_scr[h] = jnp.exp(lg * (n_col + 1.0))
            kdec_scr[h] = jnp.where(n_col < valid, jnp.exp(lg * (valid - 1.0 - n_col)), 0.0)

    @pl.when(c == 0)
    def _():
        s_scr[...] = s0_ref[...]

    for h in range(RET_HEADS):
        lg = math.log(1.0 - 2.0 ** (-5.0 - h))
        q = p_ref[:, h * RET_DK:(h + 1) * RET_DK].astype(BF16)
        k = p_ref[:, D_MODEL + h * RET_DK:D_MODEL + (h + 1) * RET_DK]
        v = p_ref[:, 2 * D_MODEL + h * RET_DV:2 * D_MODEL + (h + 1) * RET_DV].astype(BF16)
        g = p_ref[:, 2 * D_MODEL + RET_VDIM + h * RET_DV:
                  2 * D_MODEL + RET_VDIM + (h + 1) * RET_DV]
        s_prev = s_scr[h]

        scores = lax.dot_general(q, k.astype(BF16), (((1,), (1,)), ((), ())),
                                 preferred_element_type=F32)
        inner = jnp.dot((scores * decay_scr[h]).astype(BF16), v, preferred_element_type=F32)
        cross = jnp.dot(q, s_prev.astype(BF16), preferred_element_type=F32)
        o = inner + cross * pltpu.repeat(cdec_scr[h], RET_DV // LANES, 1)

        kd = (k.astype(F32) * pltpu.repeat(kdec_scr[h], RET_DK // LANES, 1)).astype(BF16)
        kv = lax.dot_general(kd, v, (((0,), (0,)), ((), ())),
                             preferred_element_type=F32)
        s_scr[h] = math.exp(lg * valid) * s_prev + kv

        gated_scr[:, h * RET_DV:(h + 1) * RET_DV] = _gn_gate(o, g).astype(BF16)

    xo_ref[...] = x_ref[...] + jnp.dot(gated_scr[...], wo_ref[...],
                                       preferred_element_type=F32)

    @pl.when(c == pl.num_programs(1) - 1)
    def _():
        sf_ref[0] = s_scr[...]


def _ret_core(proj, x, s0, w_out, *, n_seq, n_chunks, chunk, valid):
    rows = n_seq * n_chunks * chunk
    row_map = lambda b, c: (b * n_chunks + c, 0)
    state_shape = (RET_HEADS, RET_DK, RET_DV)
    return pl.pallas_call(
        functools.partial(_ret_core_kernel, chunk=chunk, valid=valid),
        grid=(n_seq, n_chunks),
        in_specs=[
            pl.BlockSpec((chunk, RET_IN), row_map),
            pl.BlockSpec((chunk, D_MODEL), row_map),
            _const_spec(state_shape),
            _const_spec((RET_VDIM, D_MODEL)),
        ],
        out_specs=[
            pl.BlockSpec((chunk, D_MODEL), row_map),
            pl.BlockSpec((1,) + state_shape, lambda b, c: (b, 0, 0, 0)),
        ],
        out_shape=[
            jax.ShapeDtypeStruct((rows, D_MODEL), F32),
            jax.ShapeDtypeStruct((n_seq,) + state_shape, F32),
        ],
        scratch_shapes=[
            pltpu.VMEM(state_shape, F32),
            pltpu.VMEM((chunk, RET_VDIM), BF16),
            pltpu.VMEM((RET_HEADS, chunk, chunk), F32),
            pltpu.VMEM((RET_HEADS, chunk, LANES), F32),
            pltpu.VMEM((RET_HEADS, chunk, LANES), F32),
        ],
        compiler_params=_params(2),
        name="ret_core",
    )(proj, x, s0, w_out)


def _ret_sample_kernel(p_ref, s_ref, o_ref, sn_ref, *, bb):
    i = pl.program_id(0)
    for j in range(bb):
        b = i * bb + j
        row = p_ref[pl.ds(b, 1), :]
        for h in range(RET_HEADS):
            gamma = 1.0 - 2.0 ** (-5.0 - h)
            q = row[:, h * RET_DK:(h + 1) * RET_DK]
            k = row[:, D_MODEL + h * RET_DK:D_MODEL + (h + 1) * RET_DK]
            v = row[:, 2 * D_MODEL + h * RET_DV:2 * D_MODEL + (h + 1) * RET_DV]
            s_prev = s_ref[j, h]
            q8 = jnp.broadcast_to(q, (8, RET_DK)).astype(BF16)
            cross = jnp.dot(q8, s_prev.astype(BF16), preferred_element_type=F32)[0:1]
            qk = jnp.sum(q * k, axis=-1, keepdims=True)
            o_ref[pl.ds(b, 1), h * RET_DV:(h + 1) * RET_DV] = qk * v + gamma * cross
            k8 = jnp.broadcast_to(k, (8, RET_DK)).astype(BF16)
            v8 = jnp.broadcast_to(v * 0.125, (8, RET_DV)).astype(BF16)
            kv = lax.dot_general(k8, v8, (((0,), (0,)), ((), ())),
                                 preferred_element_type=F32)
            sn_ref[j, h] = gamma * s_prev + kv


def _ret_sample(proj_small, state, *, bb=2):
    state_block = (bb, RET_HEADS, RET_DK, RET_DV)
    return pl.pallas_call(
        functools.partial(_ret_sample_kernel, bb=bb),
        grid=(DEC_BATCH // bb,),
        in_specs=[
            pl.BlockSpec((SMALL_BLOCK, RET_IN), lambda i: (1, 0),
                         pipeline_mode=pl.Buffered(1)),
            pl.BlockSpec(state_block, lambda i: (i, 0, 0, 0)),
        ],
        out_specs=[
            pl.BlockSpec((DEC_BATCH, RET_VDIM), lambda i: (0, 0)),
            pl.BlockSpec(state_block, lambda i: (i, 0, 0, 0)),
        ],
        out_shape=[
            jax.ShapeDtypeStruct((DEC_BATCH, RET_VDIM), F32),
            jax.ShapeDtypeStruct(state.shape, F32),
        ],
        compiler_params=_params(1),
        name="ret_sample",
    )(proj_small, state)


def _gn_outproj_kernel(o_ref, g_ref, x_ref, wo_ref, xo_ref):
    parts = []
    for h in range(RET_HEADS):
        sl = slice(h * RET_DV, (h + 1) * RET_DV)
        parts.append(_gn_gate(o_ref[:, sl], g_ref[:, sl]).astype(BF16))
    gated = jnp.concatenate(parts, axis=1)
    xo_ref[...] = x_ref[...] + jnp.dot(gated, wo_ref[...], preferred_element_type=F32)


def _gn_outproj(o, proj_small, x_small, w_out):
    return pl.pallas_call(
        _gn_outproj_kernel,
        grid=(1,),
        in_specs=[
            _const_spec((DEC_BATCH, RET_VDIM)),
            pl.BlockSpec((SMALL_BLOCK, RET_VDIM), lambda i: (1, 2),
                         pipeline_mode=pl.Buffered(1)),
            pl.BlockSpec((SMALL_BLOCK, D_MODEL), lambda i: (1, 0),
                         pipeline_mode=pl.Buffered(1)),
            _const_spec((RET_VDIM, D_MODEL)),
        ],
        out_specs=pl.BlockSpec((DEC_BATCH, D_MODEL), lambda i: (0, 0)),
        out_shape=jax.ShapeDtypeStruct((DEC_BATCH, D_MODEL), F32),
        compiler_params=_params(1),
        name="gn_outproj",
    )(o, proj_small, x_small, w_out)


def _mlp_pre(h, w_ref, c0, cw, dff, kind):
    dot = lambda off: jnp.dot(h, w_ref[:, off + c0:off + c0 + cw],
                              preferred_element_type=F32)
    if kind == "ffn":
        return dot(0), dot(dff)
    bg = dot(0)
    return dot(dff) * dot(2 * dff), bg


def _mlp_act(pre, p1, p2, other, cw_ref, cb_ref, c0, cw, kind):
    sl = slice(c0, c0 + cw)
    conv = cw_ref[0:1, sl] * p2 + cw_ref[1:2, sl] * p1 + cw_ref[2:3, sl] * pre
    if kind == "ffn":
        a = conv + cb_ref[:, sl]
        return (a / (1.0 + jnp.exp(-a))) * other
    return other * conv


def _mlp_seq_kernel(x_ref, g_ref, w_ref, cw_ref, cb_ref, wo_ref, carry0_ref, gf_ref,
                    xo_ref, st_ref, carry_scr, pre_scr, act_scr,
                    *, tm, dff, cw, kind, final_norm):
    @pl.when(pl.program_id(1) == 0)
    def _():
        carry_scr[...] = carry0_ref[...]

    h = _rms(x_ref[...], g_ref[...]).astype(BF16)
    for c0 in range(0, dff, cw):
        sl = slice(c0, c0 + cw)
        pre, other = _mlp_pre(h, w_ref, c0, cw, dff, kind)
        pre_scr[0:8, :] = carry_scr[:, sl]
        pre_scr[8:8 + tm, :] = pre
        p1 = pre_scr[7:7 + tm, :]
        p2 = pre_scr[6:6 + tm, :]
        act_scr[:, sl] = _mlp_act(pre, p1, p2, other, cw_ref, cb_ref, c0, cw, kind).astype(BF16)
        carry_scr[:, sl] = pre[tm - 8:tm, :]
    out = x_ref[...] + jnp.dot(act_scr[...], wo_ref[...], preferred_element_type=F32)
    if final_norm:
        out = _rms(out, gf_ref[...])
    xo_ref[...] = out
    st_ref[0] = carry_scr[6:8, :]


def _mlp_seq(x, g, w_in, conv_w, conv_b, w_out, carry0, g_final, *, kind, final_norm, tm=512):
    dff = w_out.shape[0]
    tiles = SEQ // tm
    row_map = lambda b, t: (b * tiles + t, 0)
    return pl.pallas_call(
        functools.partial(_mlp_seq_kernel, tm=tm, dff=dff, cw=256, kind=kind,
                          final_norm=final_norm),
        grid=(BATCH, tiles),
        in_specs=[
            pl.BlockSpec((tm, D_MODEL), row_map),
            _const_spec((1, D_MODEL)),
            _const_spec(w_in.shape),
            _const_spec((3, dff)),
            _const_spec((1, dff)),
            _const_spec((dff, D_MODEL)),
            _const_spec((8, dff)),
            _const_spec((1, D_MODEL)),
        ],
        out_specs=[
            pl.BlockSpec((tm, D_MODEL), row_map),
            pl.BlockSpec((1, 2, dff), lambda b, t: (b, 0, 0)),
        ],
        out_shape=[
            jax.ShapeDtypeStruct(x.shape, F32),
            jax.ShapeDtypeStruct((BATCH, 2, dff), F32),
        ],
        scratch_shapes=[
            pltpu.VMEM((8, dff), F32),
            pltpu.VMEM((tm + 8, 256), F32),
            pltpu.VMEM((tm, dff), BF16),
        ],
        compiler_params=_params(2),
        name="mlp_seq_" + kind,
    )(x, g, w_in, conv_w, conv_b, w_out, carry0, g_final)


def _mlp_small_kernel(x_ref, g_ref, w_ref, cw_ref, cb_ref, wo_ref, st_ref, gf_ref,
                      xo_ref, pre_ref, pre_scr, act_scr, *, dff, cw, kind, final_norm):
    nb = SMALL_BLOCK
    h = _rms(x_ref[...], g_ref[...]).astype(BF16)
    for c0 in range(0, dff, cw):
        sl = slice(c0, c0 + cw)
        pre, other = _mlp_pre(h, w_ref, c0, cw, dff, kind)
        pre_ref[:, sl] = pre
        pre_scr[0:8, :] = jnp.zeros((8, cw), F32)
        pre_scr[8:8 + nb, :] = pre[0:nb]
        act_scr[0:nb, sl] = _mlp_act(pre[0:nb], pre_scr[7:7 + nb, :], pre_scr[6:6 + nb, :],
                                     other[0:nb], cw_ref, cb_ref, c0, cw, kind).astype(BF16)
        act_scr[nb:2 * nb, sl] = _mlp_act(pre[nb:2 * nb], st_ref[:, dff + c0:dff + c0 + cw],
                                          st_ref[:, sl], other[nb:2 * nb],
                                          cw_ref, cb_ref, c0, cw, kind).astype(BF16)
    out = x_ref[...] + jnp.dot(act_scr[...], wo_ref[...], preferred_element_type=F32)
    if final_norm:
        out = _rms(out, gf_ref[...])
    xo_ref[...] = out


def _mlp_small(x, g, w_in, conv_w, conv_b, w_out, state2, g_final, *, kind, final_norm):
    dff = w_out.shape[0]
    return pl.pallas_call(
        functools.partial(_mlp_small_kernel, dff=dff, cw=256, kind=kind,
                          final_norm=final_norm),
        grid=(1,),
        in_specs=[
            _const_spec((SMALL_ROWS, D_MODEL)),
            _const_spec((1, D_MODEL)),
            _const_spec(w_in.shape),
            _const_spec((3, dff)),
            _const_spec((1, dff)),
            _const_spec((dff, D_MODEL)),
            _const_spec((DEC_BATCH, 2 * dff)),
            _const_spec((1, D_MODEL)),
        ],
        out_specs=[
            pl.BlockSpec((SMALL_ROWS, D_MODEL), lambda i: (0, 0)),
            pl.BlockSpec((SMALL_ROWS, dff), lambda i: (0, 0)),
        ],
        out_shape=[
            jax.ShapeDtypeStruct((SMALL_ROWS, D_MODEL), F32),
            jax.ShapeDtypeStruct((SMALL_ROWS, dff), F32),
        ],
        scratch_shapes=[
            pltpu.VMEM((SMALL_BLOCK + 8, 256), F32),
            pltpu.VMEM((SMALL_ROWS, dff), BF16),
        ],
        compiler_params=_params(1),
        name="mlp_small_" + kind,
    )(x, g, w_in, conv_w, conv_b, w_out, state2, g_final)


def _rope_tables(pos):
    half = RET_DK // 2
    inv = 1.0 / (ROPE_BASE ** jnp.linspace(0.0, 1.0, half, dtype=F32))
    ang = pos.astype(F32)[:, None] * inv[None, :]
    cos = jnp.repeat(jnp.cos(ang), 2, axis=1)
    sin = jnp.repeat(jnp.sin(ang), 2, axis=1)
    even = (jnp.arange(RET_DK) % 2 == 0)[None, :]
    sa = jnp.where(even, -sin, 0.0)
    sb = jnp.where(even, 0.0, sin)
    return cos, sa, sb


def kernel(x_prompt, x_sample, state_ret, state_conv, state_ffn, meta_tokens, norm_mix, norm_ffn,
           norm_final, w_ret_in, w_ret_out, w_sc_in, w_sc_conv, w_sc_out, w_ffn_in, w_ffn_conv,
           b_ffn_conv, w_ffn_out):
    nb = SMALL_BLOCK
    xb = x_prompt.reshape(BATCH * SEQ, D_MODEL)
    xs = jnp.concatenate([meta_tokens, jnp.zeros((nb - N_META, D_MODEL), F32),
                          x_sample.reshape(DEC_BATCH, D_MODEL)], axis=0)

    w_ret_in_b = w_ret_in[0].astype(BF16)
    w_ret_out_b = w_ret_out[0].astype(BF16)
    w_sc_in_b = w_sc_in[0].astype(BF16)
    w_sc_out_b = w_sc_out[0].astype(BF16)
    w_ffn_in_b = [w_ffn_in[i].astype(BF16) for i in range(2)]
    w_ffn_out_b = [w_ffn_out[i].astype(BF16) for i in range(2)]
    g_final = norm_final.reshape(1, D_MODEL)

    pos_big = N_META + jnp.arange(SEQ, dtype=jnp.int32)
    pos_small = jnp.concatenate([jnp.arange(nb, dtype=jnp.int32),
                                 jnp.full((nb,), PAST_LEN, jnp.int32)])
    tabs_big = _rope_tables(pos_big)
    tabs_small = _rope_tables(pos_small)

    g0 = norm_mix[0].reshape(1, D_MODEL)
    proj_s = _ret_inproj(xs, g0, w_ret_in_b, tabs_small, tm=SMALL_ROWS, tab_blocks=1,
                         out_dtype=F32)
    zero_state = jnp.zeros((RET_HEADS, RET_DK, RET_DV), F32)
    xs_meta, s_meta = _ret_core(proj_s, xs, zero_state, w_ret_out_b,
                                n_seq=1, n_chunks=1, chunk=nb, valid=N_META)
    o_s, ret_s = _ret_sample(proj_s, state_ret[0])
    xs_samp = _gn_outproj(o_s, proj_s, xs, w_ret_out_b)
    xs = jnp.concatenate([xs_meta, xs_samp], axis=0)

    tm_proj = 512
    proj_b = _ret_inproj(xb, g0, w_ret_in_b, tabs_big, tm=tm_proj, tab_blocks=SEQ // tm_proj,
                         out_dtype=BF16)
    chunk = 256
    xb, ret_p = _ret_core(proj_b, xb, s_meta[0], w_ret_out_b,
                          n_seq=BATCH, n_chunks=SEQ // chunk, chunk=chunk, valid=chunk)

    def ffn(i, xs, xb, final_norm):
        gi = norm_ffn[i].reshape(1, D_MODEL)
        cb = b_ffn_conv[i].reshape(1, D_FF)
        xs, pre_s = _mlp_small(xs, gi, w_ffn_in_b[i], w_ffn_conv[i], cb, w_ffn_out_b[i],
                               state_ffn[i].reshape(DEC_BATCH, 2 * D_FF), g_final,
                               kind="ffn", final_norm=final_norm)
        xb, st_p = _mlp_seq(xb, gi, w_ffn_in_b[i], w_ffn_conv[i], cb, w_ffn_out_b[i],
                            pre_s[N_META - 8:N_META], g_final, kind="ffn", final_norm=final_norm)
        st_s = jnp.stack([state_ffn[i, :, 1], pre_s[nb:]], axis=1)
        return xs, xb, st_p, st_s

    xs, xb, ffn_p0, ffn_s0 = ffn(0, xs, xb, False)

    g1 = norm_mix[1].reshape(1, D_MODEL)
    zero_b = jnp.zeros((1, D_MODEL), F32)
    xs, pre_s = _mlp_small(xs, g1, w_sc_in_b, w_sc_conv[0], zero_b, w_sc_out_b,
                           state_conv[0].reshape(DEC_BATCH, 2 * D_MODEL), g_final,
                           kind="sc", final_norm=False)
    xb, conv_p = _mlp_seq(xb, g1, w_sc_in_b, w_sc_conv[0], zero_b, w_sc_out_b,
                          pre_s[N_META - 8:N_META], g_final, kind="sc", final_norm=False)
    conv_s = jnp.stack([state_conv[0, :, 1], pre_s[nb:]], axis=1)

    xs, xb, ffn_p1, ffn_s1 = ffn(1, xs, xb, True)

    y_prompt = xb.reshape(BATCH, SEQ, D_MODEL)
    y_sample = xs[nb:].reshape(DEC_BATCH, 1, D_MODEL)
    return (y_prompt, y_sample, ret_p[None], ret_s[None], conv_p[None], conv_s[None],
            jnp.stack([ffn_p0, ffn_p1]), jnp.stack([ffn_s0, ffn_s1]))
```

```python
import functools
import math

import jax
import jax.numpy as jnp
from jax import lax
from jax.experimental import pallas as pl
from jax.experimental.pallas import tpu as pltpu

F32 = jnp.float32
BF16 = jnp.bfloat16

D_MODEL = 1024
SEQ = 2048
BATCH = 8
DEC_BATCH = 128
PAST_LEN = 16384
N_META = 16
RET_HEADS = 4
RET_DK = 256
RET_DV = 512
RET_VDIM = RET_HEADS * RET_DV
RET_IN = 2 * D_MODEL + 2 * RET_VDIM
ROPE_BASE = 10000.0
D_FF = 2816
NORM_EPS = 1e-6
GN_EPS = 1e-6

LANES = 128
SUBLANES = 8
SMALL_BLOCK = 128
SMALL_ROWS = 2 * SMALL_BLOCK
MLP_CW = 256
PROJ_CW = 512
VMEM_LIMIT = 56 * 1024 * 1024


def _params(n_axes):
    return pltpu.CompilerParams(
        dimension_semantics=("arbitrary",) * n_axes,
        vmem_limit_bytes=VMEM_LIMIT)


def _const_spec(shape, index=None):
    index = (0,) * len(shape) if index is None else index
    return pl.BlockSpec(shape, lambda *_: index, pipeline_mode=pl.Buffered(1))


def _rms(x, g):
    ms = jnp.mean(x * x, axis=-1, keepdims=True)
    return x * lax.rsqrt(ms + NORM_EPS) * g


def _rotary(zs, col, c_ref, sa_ref, sb_ref):
    sl = slice(col, col + LANES)
    return (zs * c_ref[:, sl] + pltpu.roll(zs, LANES - 1, 1) * sa_ref[:, sl]
            + pltpu.roll(zs, 1, 1) * sb_ref[:, sl])


def _ret_inproj_kernel(x_ref, g_ref, w_ref, c_ref, sa_ref, sb_ref, o_ref):
    h = _rms(x_ref[...], g_ref[...]).astype(BF16)
    for c0 in range(0, RET_IN, PROJ_CW):
        z = jnp.dot(h, w_ref[:, c0:c0 + PROJ_CW], preferred_element_type=F32)
        if c0 >= 2 * D_MODEL:
            o_ref[:, c0:c0 + PROJ_CW] = z.astype(o_ref.dtype)
            continue
        for s0 in range(0, PROJ_CW, LANES):
            r = _rotary(z[:, s0:s0 + LANES], (c0 + s0) % RET_DK, c_ref, sa_ref, sb_ref)
            if c0 >= D_MODEL:
                r = r * (RET_DK ** -0.5)
            o_ref[:, c0 + s0:c0 + s0 + LANES] = r.astype(o_ref.dtype)


def _ret_inproj(x, g, w_b, tabs, *, tm):
    rows = x.shape[0]
    tab_blocks = SEQ // tm
    tab_spec = pl.BlockSpec((tm, RET_DK), lambda i: (i % tab_blocks, 0))
    return pl.pallas_call(
        _ret_inproj_kernel,
        grid=(rows // tm,),
        in_specs=[
            pl.BlockSpec((tm, D_MODEL), lambda i: (i, 0)),
            _const_spec((None, 1, D_MODEL), (0, 0, 0)),
            _const_spec((D_MODEL, RET_IN)),
            tab_spec, tab_spec, tab_spec,
        ],
        out_specs=pl.BlockSpec((tm, RET_IN), lambda i: (i, 0)),
        out_shape=jax.ShapeDtypeStruct((rows, RET_IN), BF16),
        compiler_params=_params(1),
        name="ret_inproj",
    )(x, g, w_b, *tabs)


def _ret_inproj_small_kernel(x_ref, g_ref, w_ref, c_ref, sa_ref, sb_ref, o_ref, wb_ref, h_scr):
    c = pl.program_id(0)

    @pl.when(c == 0)
    def _():
        h_scr[...] = _rms(x_ref[...], g_ref[...]).astype(BF16)

    wb = w_ref[...].astype(BF16)
    wb_ref[...] = wb
    z = jnp.dot(h_scr[...], wb, preferred_element_type=F32)
    qk_chunks = 2 * D_MODEL // PROJ_CW

    @pl.when(c >= qk_chunks)
    def _():
        o_ref[...] = z

    @pl.when(c < qk_chunks)
    def _():
        scale = jnp.where(c >= qk_chunks // 2, RET_DK ** -0.5, 1.0)
        for s0 in range(0, PROJ_CW, LANES):
            r = _rotary(z[:, s0:s0 + LANES], s0 % RET_DK, c_ref, sa_ref, sb_ref)
            o_ref[:, s0:s0 + LANES] = r * scale


def _ret_inproj_small(x, g, w, tabs):
    return pl.pallas_call(
        _ret_inproj_small_kernel,
        grid=(RET_IN // PROJ_CW,),
        in_specs=[
            _const_spec((SMALL_ROWS, D_MODEL)),
            _const_spec((None, 1, D_MODEL), (0, 0, 0)),
            pl.BlockSpec((None, D_MODEL, PROJ_CW), lambda c: (0, 0, c)),
            _const_spec((SMALL_ROWS, RET_DK)),
            _const_spec((SMALL_ROWS, RET_DK)),
            _const_spec((SMALL_ROWS, RET_DK)),
        ],
        out_specs=[
            pl.BlockSpec((SMALL_ROWS, PROJ_CW), lambda c: (0, c)),
            pl.BlockSpec((D_MODEL, PROJ_CW), lambda c: (0, c)),
        ],
        out_shape=[
            jax.ShapeDtypeStruct((SMALL_ROWS, RET_IN), F32),
            jax.ShapeDtypeStruct((D_MODEL, RET_IN), BF16),
        ],
        scratch_shapes=[pltpu.VMEM((SMALL_ROWS, D_MODEL), BF16)],
        compiler_params=_params(1),
        name="ret_inproj_small",
    )(x, g, w, *tabs)


def _gn_gate(o, g):
    mu = jnp.mean(o, axis=-1, keepdims=True)
    d = o - mu
    var = jnp.mean(d * d, axis=-1, keepdims=True)
    on = d * lax.rsqrt(var + GN_EPS)
    gf = g.astype(F32)
    return (gf / (1.0 + jnp.exp(-gf))) * on


def _lane_tile(x, n):
    return jnp.concatenate([x] * n, axis=1)


def _ret_core_kernel(*refs, chunk, valid, cast_w):
    if cast_w:
        (p_ref, x_ref, s0_ref, wo_ref, xo_ref, sf_ref, wob_ref,
         s_scr, gated_scr, decay_scr, cdec_scr, kdec_scr) = refs
    else:
        (p_ref, x_ref, s0_ref, wo_ref, xo_ref, sf_ref,
         s_scr, gated_scr, decay_scr, cdec_scr, kdec_scr) = refs
    c = pl.program_id(1)

    @pl.when((pl.program_id(0) == 0) & (c == 0))
    def _():
        n_mat = lax.broadcasted_iota(jnp.int32, (chunk, chunk), 0)
        m_mat = lax.broadcasted_iota(jnp.int32, (chunk, chunk), 1)
        diff = (n_mat - m_mat).astype(F32)
        causal = diff >= 0
        diff = jnp.where(causal, diff, 0.0)
        n_col = lax.broadcasted_iota(jnp.int32, (chunk, LANES), 0).astype(F32)
        for h in range(RET_HEADS):
            lg = math.log(1.0 - 2.0 ** (-5.0 - h))
            decay_scr[h] = jnp.where(causal, jnp.exp(lg * diff), 0.0)
            cdec_scr[h] = jnp.exp(lg * (n_col + 1.0))
            kdec_scr[h] = jnp.where(n_col < valid, jnp.exp(lg * (valid - 1.0 - n_col)), 0.0)

    @pl.when(c == 0)
    def _():
        s_scr[...] = s0_ref[...]

    for h in range(RET_HEADS):
        lg = math.log(1.0 - 2.0 ** (-5.0 - h))
        q = p_ref[:, h * RET_DK:(h + 1) * RET_DK].astype(BF16)
        k = p_ref[:, D_MODEL + h * RET_DK:D_MODEL + (h + 1) * RET_DK]
        v = p_ref[:, 2 * D_MODEL + h * RET_DV:2 * D_MODEL + (h + 1) * RET_DV].astype(BF16)
        g = p_ref[:, 2 * D_MODEL + RET_VDIM + h * RET_DV:
                  2 * D_MODEL + RET_VDIM + (h + 1) * RET_DV]
        s_prev = s_scr[h]

        scores = lax.dot_general(q, k.astype(BF16), (((1,), (1,)), ((), ())),
                                 preferred_element_type=F32)
        inner = jnp.dot((scores * decay_scr[h]).astype(BF16), v, preferred_element_type=F32)
        cross = jnp.dot(q, s_prev.astype(BF16), preferred_element_type=F32)
        o = inner + cross * _lane_tile(cdec_scr[h], RET_DV // LANES)

        kd = (k.astype(F32) * _lane_tile(kdec_scr[h], RET_DK // LANES)).astype(BF16)
        kv = lax.dot_general(kd, v, (((0,), (0,)), ((), ())),
                             preferred_element_type=F32)
        s_scr[h] = math.exp(lg * valid) * s_prev + kv

        gated_scr[:, h * RET_DV:(h + 1) * RET_DV] = _gn_gate(o, g).astype(BF16)

    if cast_w:
        wo = wo_ref[...].astype(BF16)
        wob_ref[...] = wo
    else:
        wo = wo_ref[...]
    xo_ref[...] = x_ref[...] + jnp.dot(gated_scr[...], wo, preferred_element_type=F32)

    @pl.when(c == pl.num_programs(1) - 1)
    def _():
        sf_ref[0] = s_scr[...]


def _ret_core(proj, x, s0, w_out, *, n_seq, n_chunks, chunk, valid, cast_w):
    rows = n_seq * n_chunks * chunk
    row_map = lambda b, c: (b * n_chunks + c, 0)
    state_shape = (RET_HEADS, RET_DK, RET_DV)
    w_shape = (RET_VDIM, D_MODEL)
    out_specs = [
        pl.BlockSpec((chunk, D_MODEL), row_map),
        pl.BlockSpec((1,) + state_shape, lambda b, c: (b, 0, 0, 0)),
    ]
    out_shape = [
        jax.ShapeDtypeStruct((rows, D_MODEL), F32),
        jax.ShapeDtypeStruct((n_seq,) + state_shape, F32),
    ]
    if cast_w:
        w_spec = _const_spec((None,) + w_shape, (0, 0, 0))
        out_specs.append(pl.BlockSpec(w_shape, lambda b, c: (0, 0)))
        out_shape.append(jax.ShapeDtypeStruct(w_shape, BF16))
    else:
        w_spec = _const_spec(w_shape)
    return pl.pallas_call(
        functools.partial(_ret_core_kernel, chunk=chunk, valid=valid, cast_w=cast_w),
        grid=(n_seq, n_chunks),
        in_specs=[
            pl.BlockSpec((chunk, RET_IN), row_map),
            pl.BlockSpec((chunk, D_MODEL), row_map),
            _const_spec((None,) + state_shape, (0, 0, 0, 0)),
            w_spec,
        ],
        out_specs=out_specs,
        out_shape=out_shape,
        scratch_shapes=[
            pltpu.VMEM(state_shape, F32),
            pltpu.VMEM((chunk, RET_VDIM), BF16),
            pltpu.VMEM((RET_HEADS, chunk, chunk), F32),
            pltpu.VMEM((RET_HEADS, chunk, LANES), F32),
            pltpu.VMEM((RET_HEADS, chunk, LANES), F32),
        ],
        compiler_params=_params(2),
        name="ret_core",
    )(proj, x, s0, w_out)


def _ret_sample_kernel(p_ref, s_ref, o_ref, sn_ref, *, bb):
    i = pl.program_id(0)
    for j in range(bb):
        b = i * bb + j
        row = p_ref[pl.ds(b, 1), :]
        for h in range(RET_HEADS):
            gamma = 1.0 - 2.0 ** (-5.0 - h)
            q = row[:, h * RET_DK:(h + 1) * RET_DK]
            k = row[:, D_MODEL + h * RET_DK:D_MODEL + (h + 1) * RET_DK]
            v = row[:, 2 * D_MODEL + h * RET_DV:2 * D_MODEL + (h + 1) * RET_DV]
            s_prev = s_ref[j, h]
            q8 = jnp.broadcast_to(q, (SUBLANES, RET_DK)).astype(BF16)
            cross = jnp.dot(q8, s_prev.astype(BF16), preferred_element_type=F32)[0:1]
            qk = jnp.sum(q * k, axis=-1, keepdims=True)
            o_ref[pl.ds(b, 1), h * RET_DV:(h + 1) * RET_DV] = qk * v + gamma * cross
            k8 = jnp.broadcast_to(k, (SUBLANES, RET_DK)).astype(BF16)
            v8 = jnp.broadcast_to(v * (1.0 / SUBLANES), (SUBLANES, RET_DV)).astype(BF16)
            kv = lax.dot_general(k8, v8, (((0,), (0,)), ((), ())),
                                 preferred_element_type=F32)
            sn_ref[j, h] = gamma * s_prev + kv


def _ret_sample(proj_small, state, *, bb=2):
    state_block = (None, bb, RET_HEADS, RET_DK, RET_DV)
    return pl.pallas_call(
        functools.partial(_ret_sample_kernel, bb=bb),
        grid=(DEC_BATCH // bb,),
        in_specs=[
            _const_spec((SMALL_BLOCK, RET_IN), (1, 0)),
            pl.BlockSpec(state_block, lambda i: (0, i, 0, 0, 0)),
        ],
        out_specs=[
            pl.BlockSpec((DEC_BATCH, RET_VDIM), lambda i: (0, 0)),
            pl.BlockSpec(state_block, lambda i: (0, i, 0, 0, 0)),
        ],
        out_shape=[
            jax.ShapeDtypeStruct((DEC_BATCH, RET_VDIM), F32),
            jax.ShapeDtypeStruct(state.shape, F32),
        ],
        compiler_params=_params(1),
        name="ret_sample",
    )(proj_small, state)


def _gn_outproj_kernel(o_ref, g_ref, x_ref, wo_ref, xo_ref):
    parts = []
    for h in range(RET_HEADS):
        sl = slice(h * RET_DV, (h + 1) * RET_DV)
        parts.append(_gn_gate(o_ref[:, sl], g_ref[:, sl]).astype(BF16))
    gated = jnp.concatenate(parts, axis=1)
    xo_ref[...] = x_ref[...] + jnp.dot(gated, wo_ref[...], preferred_element_type=F32)


def _gn_outproj(o, proj_small, x_small, w_out_b):
    return pl.pallas_call(
        _gn_outproj_kernel,
        grid=(1,),
        in_specs=[
            _const_spec((DEC_BATCH, RET_VDIM)),
            _const_spec((SMALL_BLOCK, RET_VDIM), (1, 2)),
            _const_spec((SMALL_BLOCK, D_MODEL), (1, 0)),
            _const_spec((RET_VDIM, D_MODEL)),
        ],
        out_specs=pl.BlockSpec((DEC_BATCH, D_MODEL), lambda i: (0, 0)),
        out_shape=jax.ShapeDtypeStruct((DEC_BATCH, D_MODEL), F32),
        compiler_params=_params(1),
        name="gn_outproj",
    )(o, proj_small, x_small, w_out_b)


def _n_parts(kind):
    return 2 if kind == "ffn" else 3


def _mlp_pre(parts, kind):
    if kind == "ffn":
        return parts[0], parts[1]
    return parts[1] * parts[2], parts[0]


def _mlp_act(pre, p1, p2, other, cw, cb, kind):
    conv = cw[0:1] * p2 + cw[1:2] * p1 + cw[2:3] * pre
    if kind == "ffn":
        a = conv + cb
        return (a / (1.0 + jnp.exp(-a))) * other
    return other * conv


def _mlp_seq_kernel(*refs, tm, dff, kind, final_norm):
    npart = _n_parts(kind)
    x_ref, g_ref = refs[0:2]
    w_refs = refs[2:2 + npart]
    (cw_ref, cb_ref, wo_ref, carry0_ref, gf_ref,
     xo_ref, st_ref, carry_scr, pre_scr, act_scr) = refs[2 + npart:]

    @pl.when(pl.program_id(1) == 0)
    def _():
        carry_scr[...] = carry0_ref[...]

    h = _rms(x_ref[...], g_ref[...]).astype(BF16)
    for c0 in range(0, dff, MLP_CW):
        sl = slice(c0, c0 + MLP_CW)
        parts = [jnp.dot(h, w[:, sl], preferred_element_type=F32) for w in w_refs]
        pre, other = _mlp_pre(parts, kind)
        pre_scr[0:SUBLANES, :] = carry_scr[:, sl]
        pre_scr[SUBLANES:SUBLANES + tm, :] = pre
        p1 = pre_scr[SUBLANES - 1:SUBLANES - 1 + tm, :]
        p2 = pre_scr[SUBLANES - 2:SUBLANES - 2 + tm, :]
        act_scr[:, sl] = _mlp_act(pre, p1, p2, other, cw_ref[:, sl], cb_ref[:, sl],
                                  kind).astype(BF16)
        carry_scr[:, sl] = pre[tm - SUBLANES:tm, :]
    out = x_ref[...] + jnp.dot(act_scr[...], wo_ref[...], preferred_element_type=F32)
    if final_norm:
        out = _rms(out, gf_ref[...])
    xo_ref[...] = out
    st_ref[0] = carry_scr[SUBLANES - 2:SUBLANES, :]


def _mlp_seq(x, g, g_layer, w_parts, conv_w, conv_b, w_layer, w_out_b, pre_small, g_final,
             *, kind, final_norm, tm=512):
    dff = w_out_b.shape[0]
    tiles = SEQ // tm
    row_map = lambda b, t: (b * tiles + t, 0)
    return pl.pallas_call(
        functools.partial(_mlp_seq_kernel, tm=tm, dff=dff, kind=kind, final_norm=final_norm),
        grid=(BATCH, tiles),
        in_specs=[
            pl.BlockSpec((tm, D_MODEL), row_map),
            _const_spec((None, 1, D_MODEL), (g_layer, 0, 0)),
            *[_const_spec((D_MODEL, dff)) for _ in w_parts],
            _const_spec((None, 3, dff), (w_layer, 0, 0)),
            _const_spec((None, 1, dff), (w_layer, 0, 0)),
            _const_spec((dff, D_MODEL)),
            _const_spec((SUBLANES, dff), (N_META // SUBLANES - 1, 0)),
            _const_spec((None, 1, D_MODEL), (0, 0, 0)),
        ],
        out_specs=[
            pl.BlockSpec((tm, D_MODEL), row_map),
            pl.BlockSpec((1, 2, dff), lambda b, t: (b, 0, 0)),
        ],
        out_shape=[
            jax.ShapeDtypeStruct(x.shape, F32),
            jax.ShapeDtypeStruct((BATCH, 2, dff), F32),
        ],
        scratch_shapes=[
            pltpu.VMEM((SUBLANES, dff), F32),
            pltpu.VMEM((tm + SUBLANES, MLP_CW), F32),
            pltpu.VMEM((tm, dff), BF16),
        ],
        compiler_params=_params(2),
        name="mlp_seq_" + kind,
    )(x, g, *w_parts, conv_w, conv_b, w_out_b, pre_small, g_final)


def _mlp_small_kernel(*refs, kind, final_norm):
    npart = _n_parts(kind)
    nb = SMALL_BLOCK
    x_ref, g_ref = refs[0:2]
    w_refs = refs[2:2 + npart]
    (cw_ref, cb_ref, wo_ref, s0_ref, s1_ref, gf_ref) = refs[2 + npart:8 + npart]
    xo_ref, pre_ref = refs[8 + npart:10 + npart]
    wb_refs = refs[10 + npart:10 + 2 * npart]
    wob_ref, h_scr, acc_scr, pre_scr = refs[10 + 2 * npart:]
    c = pl.program_id(0)

    @pl.when(c == 0)
    def _():
        h_scr[...] = _rms(x_ref[...], g_ref[...]).astype(BF16)
        acc_scr[...] = jnp.zeros_like(acc_scr)

    parts = []
    for w_ref, wb_ref in zip(w_refs, wb_refs):
        wb = w_ref[...].astype(BF16)
        wb_ref[...] = wb
        parts.append(jnp.dot(h_scr[...], wb, preferred_element_type=F32))
    pre, other = _mlp_pre(parts, kind)
    pre_ref[...] = pre
    cw = cw_ref[...]
    cb = cb_ref[...]
    pre_scr[0:SUBLANES, :] = jnp.zeros((SUBLANES, MLP_CW), F32)
    pre_scr[SUBLANES:SUBLANES + nb, :] = pre[0:nb]
    act_meta = _mlp_act(pre[0:nb], pre_scr[SUBLANES - 1:SUBLANES - 1 + nb, :],
                        pre_scr[SUBLANES - 2:SUBLANES - 2 + nb, :], other[0:nb], cw, cb, kind)
    act_samp = _mlp_act(pre[nb:], s1_ref[...], s0_ref[...], other[nb:], cw, cb, kind)
    act = jnp.concatenate([act_meta, act_samp], axis=0).astype(BF16)
    wob = wo_ref[...].astype(BF16)
    wob_ref[...] = wob
    acc_scr[...] += jnp.dot(act, wob, preferred_element_type=F32)

    @pl.when(c == pl.num_programs(0) - 1)
    def _():
        out = x_ref[...] + acc_scr[...]
        if final_norm:
            out = _rms(out, gf_ref[...])
        xo_ref[...] = out


def _mlp_small(x, g, g_layer, w_in, conv_w, conv_b, w_out, state2, layer, g_final,
               *, kind, final_norm):
    npart = _n_parts(kind)
    dff = w_out.shape[1]
    nc = dff // MLP_CW
    chunk_map = lambda c: (0, c)
    return pl.pallas_call(
        functools.partial(_mlp_small_kernel, kind=kind, final_norm=final_norm),
        grid=(nc,),
        in_specs=[
            _const_spec((SMALL_ROWS, D_MODEL)),
            _const_spec((None, 1, D_MODEL), (g_layer, 0, 0)),
            *[pl.BlockSpec((None, D_MODEL, MLP_CW), functools.partial(
                lambda c, p: (layer, 0, p * nc + c), p=p)) for p in range(npart)],
            pl.BlockSpec((None, 3, MLP_CW), lambda c: (layer, 0, c)),
            pl.BlockSpec((None, 1, MLP_CW), lambda c: (layer, 0, c)),
            pl.BlockSpec((None, MLP_CW, D_MODEL), lambda c: (layer, c, 0)),
            pl.BlockSpec((None, DEC_BATCH, MLP_CW), lambda c: (layer, 0, c)),
            pl.BlockSpec((None, DEC_BATCH, MLP_CW), lambda c: (layer, 0, nc + c)),
            _const_spec((None, 1, D_MODEL), (0, 0, 0)),
        ],
        out_specs=[
            pl.BlockSpec((SMALL_ROWS, D_MODEL), lambda c: (0, 0)),
            pl.BlockSpec((SMALL_ROWS, MLP_CW), chunk_map),
            *[pl.BlockSpec((D_MODEL, MLP_CW), chunk_map) for _ in range(npart)],
            pl.BlockSpec((MLP_CW, D_MODEL), lambda c: (c, 0)),
        ],
        out_shape=[
            jax.ShapeDtypeStruct((SMALL_ROWS, D_MODEL), F32),
            jax.ShapeDtypeStruct((SMALL_ROWS, dff), F32),
            *[jax.ShapeDtypeStruct((D_MODEL, dff), BF16) for _ in range(npart)],
            jax.ShapeDtypeStruct((dff, D_MODEL), BF16),
        ],
        scratch_shapes=[
            pltpu.VMEM((SMALL_ROWS, D_MODEL), BF16),
            pltpu.VMEM((SMALL_ROWS, D_MODEL), F32),
            pltpu.VMEM((SMALL_BLOCK + SUBLANES, MLP_CW), F32),
        ],
        compiler_params=_params(1),
        name="mlp_small_" + kind,
    )(x, g, *([w_in] * npart), conv_w, conv_b, w_out, state2, state2, g_final)


def _rope_tables(pos):
    half = RET_DK // 2
    inv = 1.0 / (ROPE_BASE ** jnp.linspace(0.0, 1.0, half, dtype=F32))
    ang = pos.astype(F32)[:, None] * inv[None, :]
    cos = jnp.repeat(jnp.cos(ang), 2, axis=1)
    sin = jnp.repeat(jnp.sin(ang), 2, axis=1)
    even = (jnp.arange(RET_DK) % 2 == 0)[None, :]
    sa = jnp.where(even, -sin, 0.0)
    sb = jnp.where(even, 0.0, sin)
    return cos, sa, sb


def kernel(x_prompt, x_sample, state_ret, state_conv, state_ffn, meta_tokens, norm_mix, norm_ffn,
           norm_final, w_ret_in, w_ret_out, w_sc_in, w_sc_conv, w_sc_out, w_ffn_in, w_ffn_conv,
           b_ffn_conv, w_ffn_out):
    nb = SMALL_BLOCK
    xb = x_prompt.reshape(BATCH * SEQ, D_MODEL)
    xs = jnp.concatenate([meta_tokens, jnp.zeros((nb - N_META, D_MODEL), F32),
                          x_sample.reshape(DEC_BATCH, D_MODEL)], axis=0)

    g_mix = norm_mix.reshape(2, 1, D_MODEL)
    g_ffn = norm_ffn.reshape(2, 1, D_MODEL)
    g_final = norm_final.reshape(1, 1, D_MODEL)
    b_ffn = b_ffn_conv.reshape(2, 1, D_FF)
    st_ffn = state_ffn.reshape(2, DEC_BATCH, 2 * D_FF)
    st_conv = state_conv.reshape(1, DEC_BATCH, 2 * D_MODEL)

    pos_big = N_META + jnp.arange(SEQ, dtype=jnp.int32)
    pos_small = jnp.concatenate([jnp.arange(nb, dtype=jnp.int32),
                                 jnp.full((nb,), PAST_LEN, jnp.int32)])
    tabs_big = _rope_tables(pos_big)
    tabs_small = _rope_tables(pos_small)

    proj_s, w_ret_in_b = _ret_inproj_small(xs, g_mix, w_ret_in, tabs_small)
    zero_state = jnp.zeros((1, RET_HEADS, RET_DK, RET_DV), F32)
    xs_meta, s_meta, w_ret_out_b = _ret_core(
        proj_s, xs, zero_state, w_ret_out,
        n_seq=1, n_chunks=1, chunk=nb, valid=N_META, cast_w=True)
    o_s, ret_s = _ret_sample(proj_s, state_ret)
    xs_samp = _gn_outproj(o_s, proj_s, xs, w_ret_out_b)
    xs = jnp.concatenate([xs_meta, xs_samp], axis=0)

    proj_b = _ret_inproj(xb, g_mix, w_ret_in_b, tabs_big, tm=512)
    chunk = 256
    xb, ret_p = _ret_core(proj_b, xb, s_meta, w_ret_out_b,
                          n_seq=BATCH, n_chunks=SEQ // chunk, chunk=chunk, valid=chunk,
                          cast_w=False)

    def ffn(i, xs, xb, final_norm):
        xs, pre_s, w_up_b, w_gate_b, w_out_b = _mlp_small(
            xs, g_ffn, i, w_ffn_in, w_ffn_conv, b_ffn, w_ffn_out, st_ffn, i, g_final,
            kind="ffn", final_norm=final_norm)
        xb, st_p = _mlp_seq(xb, g_ffn, i, (w_up_b, w_gate_b), w_ffn_conv, b_ffn, i, w_out_b,
                            pre_s, g_final, kind="ffn", final_norm=final_norm)
        st_s = jnp.stack([state_ffn[i, :, 1], pre_s[nb:]], axis=1)
        return xs, xb, st_p, st_s

    xs, xb, ffn_p0, ffn_s0 = ffn(0, xs, xb, False)

    zero_b = jnp.zeros((1, 1, D_MODEL), F32)
    xs, pre_s, w_bg_b, w_cg_b, w_xin_b, w_sc_out_b = _mlp_small(
        xs, g_mix, 1, w_sc_in, w_sc_conv, zero_b, w_sc_out, st_conv, 0, g_final,
        kind="sc", final_norm=False)
    xb, conv_p = _mlp_seq(xb, g_mix, 1, (w_bg_b, w_cg_b, w_xin_b), w_sc_conv, zero_b, 0,
                          w_sc_out_b, pre_s, g_final, kind="sc", final_norm=False)
    conv_s = jnp.stack([state_conv[0, :, 1], pre_s[nb:]], axis=1)

    xs, xb, ffn_p1, ffn_s1 = ffn(1, xs, xb, True)

    y_prompt = xb.reshape(BATCH, SEQ, D_MODEL)
    y_sample = xs[nb:].reshape(DEC_BATCH, 1, D_MODEL)
    return (y_prompt, y_sample, ret_p[None], ret_s, conv_p[None], conv_s[None],
            jnp.stack([ffn_p0, ffn_p1]), jnp.stack([ffn_s0, ffn_s1]))
```

```python
import functools
import math

import jax
import jax.numpy as jnp
from jax import lax
from jax.experimental import pallas as pl
from jax.experimental.pallas import tpu as pltpu

F32 = jnp.float32
BF16 = jnp.bfloat16

D_MODEL = 1024
SEQ = 2048
BATCH = 8
DEC_BATCH = 128
PAST_LEN = 16384
N_META = 16
RET_HEADS = 4
RET_DK = 256
RET_DV = 512
RET_VDIM = RET_HEADS * RET_DV
RET_IN = 2 * D_MODEL + 2 * RET_VDIM
ROPE_BASE = 10000.0
D_FF = 2816
NORM_EPS = 1e-6
GN_EPS = 1e-6

LANES = 128
SUBLANES = 8
SMALL_BLOCK = 128
SMALL_ROWS = 2 * SMALL_BLOCK
MLP_CW = 256
PROJ_CW = 512
VMEM_LIMIT = 56 * 1024 * 1024


def _params(n_axes):
    return pltpu.CompilerParams(
        dimension_semantics=("arbitrary",) * n_axes,
        vmem_limit_bytes=VMEM_LIMIT)


def _const_spec(shape, index=None):
    index = (0,) * len(shape) if index is None else index
    return pl.BlockSpec(shape, lambda *_: index, pipeline_mode=pl.Buffered(1))


def _rms(x, g):
    ms = jnp.mean(x * x, axis=-1, keepdims=True)
    return x * lax.rsqrt(ms + NORM_EPS) * g


def _rotary(zs, col, c_ref, sa_ref, sb_ref):
    sl = slice(col, col + LANES)
    return (zs * c_ref[:, sl] + pltpu.roll(zs, LANES - 1, 1) * sa_ref[:, sl]
            + pltpu.roll(zs, 1, 1) * sb_ref[:, sl])


def _sample_state_update(p_ref, s_ref, o_ref, sn_ref, first, count):
    for j in range(count):
        b = first + j
        row = p_ref[pl.ds(b, 1), :]
        for h in range(RET_HEADS):
            gamma = 1.0 - 2.0 ** (-5.0 - h)
            q = row[:, h * RET_DK:(h + 1) * RET_DK]
            k = row[:, D_MODEL + h * RET_DK:D_MODEL + (h + 1) * RET_DK]
            v = row[:, 2 * D_MODEL + h * RET_DV:2 * D_MODEL + (h + 1) * RET_DV]
            s_prev = s_ref[j, h]
            q8 = jnp.broadcast_to(q, (SUBLANES, RET_DK)).astype(BF16)
            cross = jnp.dot(q8, s_prev.astype(BF16), preferred_element_type=F32)[0:1]
            qk = jnp.sum(q * k, axis=-1, keepdims=True)
            o_ref[pl.ds(b, 1), h * RET_DV:(h + 1) * RET_DV] = qk * v + gamma * cross
            k8 = jnp.broadcast_to(k, (SUBLANES, RET_DK)).astype(BF16)
            v8 = jnp.broadcast_to(v * (1.0 / SUBLANES), (SUBLANES, RET_DV)).astype(BF16)
            kv = lax.dot_general(k8, v8, (((0,), (0,)), ((), ())),
                                 preferred_element_type=F32)
            sn_ref[j, h] = gamma * s_prev + kv


def _ret_inproj_kernel(x_ref, g_ref, w_ref, c_ref, sa_ref, sb_ref, ps_ref, s_ref,
                       o_ref, os_ref, sn_ref, *, bb):
    _sample_state_update(ps_ref, s_ref, os_ref, sn_ref, pl.program_id(0) * bb, bb)
    h = _rms(x_ref[...], g_ref[...]).astype(BF16)
    for c0 in range(0, RET_IN, PROJ_CW):
        z = jnp.dot(h, w_ref[:, c0:c0 + PROJ_CW], preferred_element_type=F32)
        if c0 >= 2 * D_MODEL:
            o_ref[:, c0:c0 + PROJ_CW] = z.astype(o_ref.dtype)
            continue
        for s0 in range(0, PROJ_CW, LANES):
            r = _rotary(z[:, s0:s0 + LANES], (c0 + s0) % RET_DK, c_ref, sa_ref, sb_ref)
            if c0 >= D_MODEL:
                r = r * (RET_DK ** -0.5)
            o_ref[:, c0 + s0:c0 + s0 + LANES] = r.astype(o_ref.dtype)


def _ret_inproj(x, g, w_b, tabs, proj_small, state, *, tm):
    rows = x.shape[0]
    steps = rows // tm
    bb = DEC_BATCH // steps
    tab_blocks = SEQ // tm
    tab_spec = pl.BlockSpec((tm, RET_DK), lambda i: (i % tab_blocks, 0))
    state_spec = pl.BlockSpec((None, bb, RET_HEADS, RET_DK, RET_DV), lambda i: (0, i, 0, 0, 0))
    return pl.pallas_call(
        functools.partial(_ret_inproj_kernel, bb=bb),
        grid=(steps,),
        in_specs=[
            pl.BlockSpec((tm, D_MODEL), lambda i: (i, 0)),
            _const_spec((None, 1, D_MODEL), (0, 0, 0)),
            _const_spec((D_MODEL, RET_IN)),
            tab_spec, tab_spec, tab_spec,
            _const_spec((SMALL_BLOCK, RET_IN), (1, 0)),
            state_spec,
        ],
        out_specs=[
            pl.BlockSpec((tm, RET_IN), lambda i: (i, 0)),
            pl.BlockSpec((DEC_BATCH, RET_VDIM), lambda i: (0, 0)),
            state_spec,
        ],
        out_shape=[
            jax.ShapeDtypeStruct((rows, RET_IN), BF16),
            jax.ShapeDtypeStruct((DEC_BATCH, RET_VDIM), F32),
            jax.ShapeDtypeStruct(state.shape, F32),
        ],
        compiler_params=_params(1),
        name="ret_inproj",
    )(x, g, w_b, *tabs, proj_small, state)


def _ret_inproj_small_kernel(x_ref, g_ref, w_ref, c_ref, sa_ref, sb_ref, o_ref, wb_ref, h_scr):
    c = pl.program_id(0)

    @pl.when(c == 0)
    def _():
        h_scr[...] = _rms(x_ref[...], g_ref[...]).astype(BF16)

    wb = w_ref[...].astype(BF16)
    wb_ref[...] = wb
    z = jnp.dot(h_scr[...], wb, preferred_element_type=F32)
    qk_chunks = 2 * D_MODEL // PROJ_CW

    @pl.when(c >= qk_chunks)
    def _():
        o_ref[...] = z

    @pl.when(c < qk_chunks)
    def _():
        scale = jnp.where(c >= qk_chunks // 2, RET_DK ** -0.5, 1.0)
        for s0 in range(0, PROJ_CW, LANES):
            r = _rotary(z[:, s0:s0 + LANES], s0 % RET_DK, c_ref, sa_ref, sb_ref)
            o_ref[:, s0:s0 + LANES] = r * scale


def _ret_inproj_small(x, g, w, tabs):
    return pl.pallas_call(
        _ret_inproj_small_kernel,
        grid=(RET_IN // PROJ_CW,),
        in_specs=[
            _const_spec((SMALL_ROWS, D_MODEL)),
            _const_spec((None, 1, D_MODEL), (0, 0, 0)),
            pl.BlockSpec((None, D_MODEL, PROJ_CW), lambda c: (0, 0, c)),
            _const_spec((SMALL_ROWS, RET_DK)),
            _const_spec((SMALL_ROWS, RET_DK)),
            _const_spec((SMALL_ROWS, RET_DK)),
        ],
        out_specs=[
            pl.BlockSpec((SMALL_ROWS, PROJ_CW), lambda c: (0, c)),
            pl.BlockSpec((D_MODEL, PROJ_CW), lambda c: (0, c)),
        ],
        out_shape=[
            jax.ShapeDtypeStruct((SMALL_ROWS, RET_IN), F32),
            jax.ShapeDtypeStruct((D_MODEL, RET_IN), BF16),
        ],
        scratch_shapes=[pltpu.VMEM((SMALL_ROWS, D_MODEL), BF16)],
        compiler_params=_params(1),
        name="ret_inproj_small",
    )(x, g, w, *tabs)


def _gn_gate(o, g):
    mu = jnp.mean(o, axis=-1, keepdims=True)
    d = o - mu
    var = jnp.mean(d * d, axis=-1, keepdims=True)
    on = d * lax.rsqrt(var + GN_EPS)
    gf = g.astype(F32)
    return (gf / (1.0 + jnp.exp(-gf))) * on


def _lane_tile(x, n):
    return jnp.concatenate([x] * n, axis=1)


def _ret_core_kernel(*refs, chunk, valid, cast_w):
    if cast_w:
        (p_ref, x_ref, s0_ref, wo_ref, xo_ref, sf_ref, wob_ref,
         s_scr, gated_scr, decay_scr, cdec_scr, kdec_scr) = refs
    else:
        (p_ref, x_ref, s0_ref, wo_ref, xo_ref, sf_ref,
         s_scr, gated_scr, decay_scr, cdec_scr, kdec_scr) = refs
    c = pl.program_id(1)

    @pl.when((pl.program_id(0) == 0) & (c == 0))
    def _():
        n_mat = lax.broadcasted_iota(jnp.int32, (chunk, chunk), 0)
        m_mat = lax.broadcasted_iota(jnp.int32, (chunk, chunk), 1)
        diff = (n_mat - m_mat).astype(F32)
        causal = diff >= 0
        diff = jnp.where(causal, diff, 0.0)
        n_col = lax.broadcasted_iota(jnp.int32, (chunk, LANES), 0).astype(F32)
        for h in range(RET_HEADS):
            lg = math.log(1.0 - 2.0 ** (-5.0 - h))
            decay_scr[h] = jnp.where(causal, jnp.exp(lg * diff), 0.0)
            cdec_scr[h] = jnp.exp(lg * (n_col + 1.0))
            kdec_scr[h] = jnp.where(n_col < valid, jnp.exp(lg * (valid - 1.0 - n_col)), 0.0)

    @pl.when(c == 0)
    def _():
        s_scr[...] = s0_ref[...]

    for h in range(RET_HEADS):
        lg = math.log(1.0 - 2.0 ** (-5.0 - h))
        q = p_ref[:, h * RET_DK:(h + 1) * RET_DK].astype(BF16)
        k = p_ref[:, D_MODEL + h * RET_DK:D_MODEL + (h + 1) * RET_DK]
        v = p_ref[:, 2 * D_MODEL + h * RET_DV:2 * D_MODEL + (h + 1) * RET_DV].astype(BF16)
        g = p_ref[:, 2 * D_MODEL + RET_VDIM + h * RET_DV:
                  2 * D_MODEL + RET_VDIM + (h + 1) * RET_DV]
        s_prev = s_scr[h]

        scores = lax.dot_general(q, k.astype(BF16), (((1,), (1,)), ((), ())),
                                 preferred_element_type=F32)
        inner = jnp.dot((scores * decay_scr[h]).astype(BF16), v, preferred_element_type=F32)
        cross = jnp.dot(q, s_prev.astype(BF16), preferred_element_type=F32)
        o = inner + cross * _lane_tile(cdec_scr[h], RET_DV // LANES)

        kd = (k.astype(F32) * _lane_tile(kdec_scr[h], RET_DK // LANES)).astype(BF16)
        kv = lax.dot_general(kd, v, (((0,), (0,)), ((), ())),
                             preferred_element_type=F32)
        s_scr[h] = math.exp(lg * valid) * s_prev + kv

        gated_scr[:, h * RET_DV:(h + 1) * RET_DV] = _gn_gate(o, g).astype(BF16)

    if cast_w:
        wo = wo_ref[...].astype(BF16)
        wob_ref[...] = wo
    else:
        wo = wo_ref[...]
    xo_ref[...] = x_ref[...] + jnp.dot(gated_scr[...], wo, preferred_element_type=F32)

    @pl.when(c == pl.num_programs(1) - 1)
    def _():
        sf_ref[0] = s_scr[...]


def _ret_core(proj, x, s0, w_out, *, n_seq, n_chunks, chunk, valid, cast_w):
    rows = n_seq * n_chunks * chunk
    row_map = lambda b, c: (b * n_chunks + c, 0)
    state_shape = (RET_HEADS, RET_DK, RET_DV)
    w_shape = (RET_VDIM, D_MODEL)
    out_specs = [
        pl.BlockSpec((chunk, D_MODEL), row_map),
        pl.BlockSpec((1,) + state_shape, lambda b, c: (b, 0, 0, 0)),
    ]
    out_shape = [
        jax.ShapeDtypeStruct((rows, D_MODEL), F32),
        jax.ShapeDtypeStruct((n_seq,) + state_shape, F32),
    ]
    if cast_w:
        w_spec = _const_spec((None,) + w_shape, (0, 0, 0))
        out_specs.append(pl.BlockSpec(w_shape, lambda b, c: (0, 0)))
        out_shape.append(jax.ShapeDtypeStruct(w_shape, BF16))
    else:
        w_spec = _const_spec(w_shape)
    return pl.pallas_call(
        functools.partial(_ret_core_kernel, chunk=chunk, valid=valid, cast_w=cast_w),
        grid=(n_seq, n_chunks),
        in_specs=[
            pl.BlockSpec((chunk, RET_IN), row_map),
            pl.BlockSpec((chunk, D_MODEL), row_map),
            _const_spec((None,) + state_shape, (0, 0, 0, 0)),
            w_spec,
        ],
        out_specs=out_specs,
        out_shape=out_shape,
        scratch_shapes=[
            pltpu.VMEM(state_shape, F32),
            pltpu.VMEM((chunk, RET_VDIM), BF16),
            pltpu.VMEM((RET_HEADS, chunk, chunk), F32),
            pltpu.VMEM((RET_HEADS, chunk, LANES), F32),
            pltpu.VMEM((RET_HEADS, chunk, LANES), F32),
        ],
        compiler_params=_params(2),
        name="ret_core",
    )(proj, x, s0, w_out)


def _gn_outproj_kernel(o_ref, g_ref, x_ref, wo_ref, xo_ref):
    parts = []
    for h in range(RET_HEADS):
        sl = slice(h * RET_DV, (h + 1) * RET_DV)
        parts.append(_gn_gate(o_ref[:, sl], g_ref[:, sl]).astype(BF16))
    gated = jnp.concatenate(parts, axis=1)
    xo_ref[...] = x_ref[...] + jnp.dot(gated, wo_ref[...], preferred_element_type=F32)


def _gn_outproj(o, proj_small, x_small, w_out_b):
    return pl.pallas_call(
        _gn_outproj_kernel,
        grid=(1,),
        in_specs=[
            _const_spec((DEC_BATCH, RET_VDIM)),
            _const_spec((SMALL_BLOCK, RET_VDIM), (1, 2)),
            _const_spec((SMALL_BLOCK, D_MODEL), (1, 0)),
            _const_spec((RET_VDIM, D_MODEL)),
        ],
        out_specs=pl.BlockSpec((DEC_BATCH, D_MODEL), lambda i: (0, 0)),
        out_shape=jax.ShapeDtypeStruct((DEC_BATCH, D_MODEL), F32),
        compiler_params=_params(1),
        name="gn_outproj",
    )(o, proj_small, x_small, w_out_b)


def _n_parts(kind):
    return 2 if kind == "ffn" else 3


def _mlp_pre(parts, kind):
    if kind == "ffn":
        return parts[0], parts[1]
    return parts[1] * parts[2], parts[0]


def _mlp_act(pre, p1, p2, other, cw, cb, kind):
    conv = cw[0:1] * p2 + cw[1:2] * p1 + cw[2:3] * pre
    if kind == "ffn":
        a = conv + cb
        return (a / (1.0 + jnp.exp(-a))) * other
    return other * conv


def _mlp_seq_kernel(*refs, tm, dff, kind, final_norm):
    npart = _n_parts(kind)
    x_ref, g_ref = refs[0:2]
    w_refs = refs[2:2 + npart]
    (cw_ref, cb_ref, wo_ref, carry0_ref, gf_ref,
     xo_ref, st_ref, carry_scr, pre_scr, act_scr) = refs[2 + npart:]

    @pl.when(pl.program_id(1) == 0)
    def _():
        carry_scr[...] = carry0_ref[...]

    h = _rms(x_ref[...], g_ref[...]).astype(BF16)
    for c0 in range(0, dff, MLP_CW):
        sl = slice(c0, c0 + MLP_CW)
        parts = [jnp.dot(h, w[:, sl], preferred_element_type=F32) for w in w_refs]
        pre, other = _mlp_pre(parts, kind)
        pre_scr[0:SUBLANES, :] = carry_scr[:, sl]
        pre_scr[SUBLANES:SUBLANES + tm, :] = pre
        p1 = pre_scr[SUBLANES - 1:SUBLANES - 1 + tm, :]
        p2 = pre_scr[SUBLANES - 2:SUBLANES - 2 + tm, :]
        act_scr[:, sl] = _mlp_act(pre, p1, p2, other, cw_ref[:, sl], cb_ref[:, sl],
                                  kind).astype(BF16)
        carry_scr[:, sl] = pre[tm - SUBLANES:tm, :]
    out = x_ref[...] + jnp.dot(act_scr[...], wo_ref[...], preferred_element_type=F32)
    if final_norm:
        out = _rms(out, gf_ref[...])
    xo_ref[...] = out
    st_ref[0] = carry_scr[SUBLANES - 2:SUBLANES, :]


def _mlp_seq(x, g, g_layer, w_parts, conv_w, conv_b, w_layer, w_out_b, pre_small, g_final,
             *, kind, final_norm, tm=512):
    dff = w_out_b.shape[0]
    tiles = SEQ // tm
    row_map = lambda b, t: (b * tiles + t, 0)
    return pl.pallas_call(
        functools.partial(_mlp_seq_kernel, tm=tm, dff=dff, kind=kind, final_norm=final_norm),
        grid=(BATCH, tiles),
        in_specs=[
            pl.BlockSpec((tm, D_MODEL), row_map),
            _const_spec((None, 1, D_MODEL), (g_layer, 0, 0)),
            *[_const_spec((D_MODEL, dff)) for _ in w_parts],
            _const_spec((None, 3, dff), (w_layer, 0, 0)),
            _const_spec((None, 1, dff), (w_layer, 0, 0)),
            _const_spec((dff, D_MODEL)),
            _const_spec((SUBLANES, dff), (N_META // SUBLANES - 1, 0)),
            _const_spec((None, 1, D_MODEL), (0, 0, 0)),
        ],
        out_specs=[
            pl.BlockSpec((tm, D_MODEL), row_map),
            pl.BlockSpec((1, 2, dff), lambda b, t: (b, 0, 0)),
        ],
        out_shape=[
            jax.ShapeDtypeStruct(x.shape, F32),
            jax.ShapeDtypeStruct((BATCH, 2, dff), F32),
        ],
        scratch_shapes=[
            pltpu.VMEM((SUBLANES, dff), F32),
            pltpu.VMEM((tm + SUBLANES, MLP_CW), F32),
            pltpu.VMEM((tm, dff), BF16),
        ],
        compiler_params=_params(2),
        name="mlp_seq_" + kind,
    )(x, g, *w_parts, conv_w, conv_b, w_out_b, pre_small, g_final)


def _mlp_small_kernel(*refs, kind, final_norm):
    npart = _n_parts(kind)
    nb = SMALL_BLOCK
    x_ref, g_ref = refs[0:2]
    w_refs = refs[2:2 + npart]
    (cw_ref, cb_ref, wo_ref, s0_ref, s1_ref, gf_ref) = refs[2 + npart:8 + npart]
    xo_ref, pre_ref = refs[8 + npart:10 + npart]
    wb_refs = refs[10 + npart:10 + 2 * npart]
    wob_ref, h_scr, acc_scr, pre_scr = refs[10 + 2 * npart:]
    c = pl.program_id(0)

    @pl.when(c == 0)
    def _():
        h_scr[...] = _rms(x_ref[...], g_ref[...]).astype(BF16)
        acc_scr[...] = jnp.zeros_like(acc_scr)

    parts = []
    for w_ref, wb_ref in zip(w_refs, wb_refs):
        wb = w_ref[...].astype(BF16)
        wb_ref[...] = wb
        parts.append(jnp.dot(h_scr[...], wb, preferred_element_type=F32))
    pre, other = _mlp_pre(parts, kind)
    pre_ref[...] = pre
    cw = cw_ref[...]
    cb = cb_ref[...]
    pre_scr[0:SUBLANES, :] = jnp.zeros((SUBLANES, MLP_CW), F32)
    pre_scr[SUBLANES:SUBLANES + nb, :] = pre[0:nb]
    act_meta = _mlp_act(pre[0:nb], pre_scr[SUBLANES - 1:SUBLANES - 1 + nb, :],
                        pre_scr[SUBLANES - 2:SUBLANES - 2 + nb, :], other[0:nb], cw, cb, kind)
    act_samp = _mlp_act(pre[nb:], s1_ref[...], s0_ref[...], other[nb:], cw, cb, kind)
    act = jnp.concatenate([act_meta, act_samp], axis=0).astype(BF16)
    wob = wo_ref[...].astype(BF16)
    wob_ref[...] = wob
    acc_scr[...] += jnp.dot(act, wob, preferred_element_type=F32)

    @pl.when(c == pl.num_programs(0) - 1)
    def _():
        out = x_ref[...] + acc_scr[...]
        if final_norm:
            out = _rms(out, gf_ref[...])
        xo_ref[...] = out


def _mlp_small(x, g, g_layer, w_in, conv_w, conv_b, w_out, state2, layer, g_final,
               *, kind, final_norm):
    npart = _n_parts(kind)
    dff = w_out.shape[1]
    nc = dff // MLP_CW
    chunk_map = lambda c: (0, c)
    return pl.pallas_call(
        functools.partial(_mlp_small_kernel, kind=kind, final_norm=final_norm),
        grid=(nc,),
        in_specs=[
            _const_spec((SMALL_ROWS, D_MODEL)),
            _const_spec((None, 1, D_MODEL), (g_layer, 0, 0)),
            *[pl.BlockSpec((None, D_MODEL, MLP_CW), functools.partial(
                lambda c, p: (layer, 0, p * nc + c), p=p)) for p in range(npart)],
            pl.BlockSpec((None, 3, MLP_CW), lambda c: (layer, 0, c)),
            pl.BlockSpec((None, 1, MLP_CW), lambda c: (layer, 0, c)),
            pl.BlockSpec((None, MLP_CW, D_MODEL), lambda c: (layer, c, 0)),
            pl.BlockSpec((None, DEC_BATCH, MLP_CW), lambda c: (layer, 0, c)),
            pl.BlockSpec((None, DEC_BATCH, MLP_CW), lambda c: (layer, 0, nc + c)),
            _const_spec((None, 1, D_MODEL), (0, 0, 0)),
        ],
        out_specs=[
            pl.BlockSpec((SMALL_ROWS, D_MODEL), lambda c: (0, 0)),
            pl.BlockSpec((SMALL_ROWS, MLP_CW), chunk_map),
            *[pl.BlockSpec((D_MODEL, MLP_CW), chunk_map) for _ in range(npart)],
            pl.BlockSpec((MLP_CW, D_MODEL), lambda c: (c, 0)),
        ],
        out_shape=[
            jax.ShapeDtypeStruct((SMALL_ROWS, D_MODEL), F32),
            jax.ShapeDtypeStruct((SMALL_ROWS, dff), F32),
            *[jax.ShapeDtypeStruct((D_MODEL, dff), BF16) for _ in range(npart)],
            jax.ShapeDtypeStruct((dff, D_MODEL), BF16),
        ],
        scratch_shapes=[
            pltpu.VMEM((SMALL_ROWS, D_MODEL), BF16),
            pltpu.VMEM((SMALL_ROWS, D_MODEL), F32),
            pltpu.VMEM((SMALL_BLOCK + SUBLANES, MLP_CW), F32),
        ],
        compiler_params=_params(1),
        name="mlp_small_" + kind,
    )(x, g, *([w_in] * npart), conv_w, conv_b, w_out, state2, state2, g_final)


def _rope_tables(pos):
    half = RET_DK // 2
    inv = 1.0 / (ROPE_BASE ** jnp.linspace(0.0, 1.0, half, dtype=F32))
    ang = pos.astype(F32)[:, None] * inv[None, :]
    cos = jnp.repeat(jnp.cos(ang), 2, axis=1)
    sin = jnp.repeat(jnp.sin(ang), 2, axis=1)
    even = (jnp.arange(RET_DK) % 2 == 0)[None, :]
    sa = jnp.where(even, -sin, 0.0)
    sb = jnp.where(even, 0.0, sin)
    return cos, sa, sb


def kernel(x_prompt, x_sample, state_ret, state_conv, state_ffn, meta_tokens, norm_mix, norm_ffn,
           norm_final, w_ret_in, w_ret_out, w_sc_in, w_sc_conv, w_sc_out, w_ffn_in, w_ffn_conv,
           b_ffn_conv, w_ffn_out):
    nb = SMALL_BLOCK
    xb = x_prompt.reshape(BATCH * SEQ, D_MODEL)
    xs = jnp.concatenate([meta_tokens, jnp.zeros((nb - N_META, D_MODEL), F32),
                          x_sample.reshape(DEC_BATCH, D_MODEL)], axis=0)

    g_mix = norm_mix.reshape(2, 1, D_MODEL)
    g_ffn = norm_ffn.reshape(2, 1, D_MODEL)
    g_final = norm_final.reshape(1, 1, D_MODEL)
    b_ffn = b_ffn_conv.reshape(2, 1, D_FF)
    st_ffn = state_ffn.reshape(2, DEC_BATCH, 2 * D_FF)
    st_conv = state_conv.reshape(1, DEC_BATCH, 2 * D_MODEL)

    pos_big = N_META + jnp.arange(SEQ, dtype=jnp.int32)
    pos_small = jnp.concatenate([jnp.arange(nb, dtype=jnp.int32),
                                 jnp.full((nb,), PAST_LEN, jnp.int32)])
    tabs_big = _rope_tables(pos_big)
    tabs_small = _rope_tables(pos_small)

    proj_s, w_ret_in_b = _ret_inproj_small(xs, g_mix, w_ret_in, tabs_small)
    zero_state = jnp.zeros((1, RET_HEADS, RET_DK, RET_DV), F32)
    xs_meta, s_meta, w_ret_out_b = _ret_core(
        proj_s, xs, zero_state, w_ret_out,
        n_seq=1, n_chunks=1, chunk=nb, valid=N_META, cast_w=True)
    proj_b, o_s, ret_s = _ret_inproj(xb, g_mix, w_ret_in_b, tabs_big, proj_s, state_ret, tm=256)
    xs_samp = _gn_outproj(o_s, proj_s, xs, w_ret_out_b)
    xs = jnp.concatenate([xs_meta, xs_samp], axis=0)
    chunk = 256
    xb, ret_p = _ret_core(proj_b, xb, s_meta, w_ret_out_b,
                          n_seq=BATCH, n_chunks=SEQ // chunk, chunk=chunk, valid=chunk,
                          cast_w=False)

    def ffn(i, xs, xb, final_norm):
        xs, pre_s, w_up_b, w_gate_b, w_out_b = _mlp_small(
            xs, g_ffn, i, w_ffn_in, w_ffn_conv, b_ffn, w_ffn_out, st_ffn, i, g_final,
            kind="ffn", final_norm=final_norm)
        xb, st_p = _mlp_seq(xb, g_ffn, i, (w_up_b, w_gate_b), w_ffn_conv, b_ffn, i, w_out_b,
                            pre_s, g_final, kind="ffn", final_norm=final_norm)
        st_s = jnp.stack([state_ffn[i, :, 1], pre_s[nb:]], axis=1)
        return xs, xb, st_p, st_s

    xs, xb, ffn_p0, ffn_s0 = ffn(0, xs, xb, False)

    zero_b = jnp.zeros((1, 1, D_MODEL), F32)
    xs, pre_s, w_bg_b, w_cg_b, w_xin_b, w_sc_out_b = _mlp_small(
        xs, g_mix, 1, w_sc_in, w_sc_conv, zero_b, w_sc_out, st_conv, 0, g_final,
        kind="sc", final_norm=False)
    xb, conv_p = _mlp_seq(xb, g_mix, 1, (w_bg_b, w_cg_b, w_xin_b), w_sc_conv, zero_b, 0,
                          w_sc_out_b, pre_s, g_final, kind="sc", final_norm=False)
    conv_s = jnp.stack([state_conv[0, :, 1], pre_s[nb:]], axis=1)

    xs, xb, ffn_p1, ffn_s1 = ffn(1, xs, xb, True)

    y_prompt = xb.reshape(BATCH, SEQ, D_MODEL)
    y_sample = xs[nb:].reshape(DEC_BATCH, 1, D_MODEL)
    return (y_prompt, y_sample, ret_p[None], ret_s, conv_p[None], conv_s[None],
            jnp.stack([ffn_p0, ffn_p1]), jnp.stack([ffn_s0, ffn_s1]))
```

```python
import functools
import math

import jax
import jax.numpy as jnp
from jax import lax
from jax.experimental import pallas as pl
from jax.experimental.pallas import tpu as pltpu

F32 = jnp.float32
BF16 = jnp.bfloat16

D_MODEL = 1024
SEQ = 2048
BATCH = 8
DEC_BATCH = 128
PAST_LEN = 16384
N_META = 16
RET_HEADS = 4
RET_DK = 256
RET_DV = 512
RET_VDIM = RET_HEADS * RET_DV
RET_IN = 2 * D_MODEL + 2 * RET_VDIM
ROPE_BASE = 10000.0
D_FF = 2816
NORM_EPS = 1e-6
GN_EPS = 1e-6

LANES = 128
SUBLANES = 8
SMALL_BLOCK = 128
SMALL_ROWS = 2 * SMALL_BLOCK
MLP_CW = 256
PROJ_CW = 512
VMEM_LIMIT = 56 * 1024 * 1024
RET_LAYER_VMEM_LIMIT = 60 * 1024 * 1024


def _params(n_axes):
    return pltpu.CompilerParams(
        dimension_semantics=("arbitrary",) * n_axes,
        vmem_limit_bytes=VMEM_LIMIT)


def _const_spec(shape, index=None):
    index = (0,) * len(shape) if index is None else index
    return pl.BlockSpec(shape, lambda *_: index, pipeline_mode=pl.Buffered(1))


def _rms(x, g):
    ms = jnp.mean(x * x, axis=-1, keepdims=True)
    return x * lax.rsqrt(ms + NORM_EPS) * g


def _rotary(zs, col, c_ref, sa_ref, sb_ref):
    sl = slice(col, col + LANES)
    return (zs * c_ref[:, sl] + pltpu.roll(zs, LANES - 1, 1) * sa_ref[:, sl]
            + pltpu.roll(zs, 1, 1) * sb_ref[:, sl])


def _sample_state_update(p_ref, s_ref, o_ref, sn_ref, first, count):
    for j in range(count):
        b = first + j
        row = p_ref[pl.ds(b, 1), :]
        for h in range(RET_HEADS):
            gamma = 1.0 - 2.0 ** (-5.0 - h)
            q = row[:, h * RET_DK:(h + 1) * RET_DK]
            k = row[:, D_MODEL + h * RET_DK:D_MODEL + (h + 1) * RET_DK]
            v = row[:, 2 * D_MODEL + h * RET_DV:2 * D_MODEL + (h + 1) * RET_DV]
            s_prev = s_ref[j, h]
            q8 = jnp.broadcast_to(q, (SUBLANES, RET_DK)).astype(BF16)
            cross = jnp.dot(q8, s_prev.astype(BF16), preferred_element_type=F32)[0:1]
            qk = jnp.sum(q * k, axis=-1, keepdims=True)
            o_ref[pl.ds(b, 1), h * RET_DV:(h + 1) * RET_DV] = qk * v + gamma * cross
            k8 = jnp.broadcast_to(k, (SUBLANES, RET_DK)).astype(BF16)
            v8 = jnp.broadcast_to(v * (1.0 / SUBLANES), (SUBLANES, RET_DV)).astype(BF16)
            kv = lax.dot_general(k8, v8, (((0,), (0,)), ((), ())),
                                 preferred_element_type=F32)
            sn_ref[j, h] = gamma * s_prev + kv


def _ret_inproj_small_kernel(x_ref, g_ref, w_ref, c_ref, sa_ref, sb_ref, o_ref, wb_ref, h_scr):
    c = pl.program_id(0)

    @pl.when(c == 0)
    def _():
        h_scr[...] = _rms(x_ref[...], g_ref[...]).astype(BF16)

    wb = w_ref[...].astype(BF16)
    wb_ref[...] = wb
    z = jnp.dot(h_scr[...], wb, preferred_element_type=F32)
    qk_chunks = 2 * D_MODEL // PROJ_CW

    @pl.when(c >= qk_chunks)
    def _():
        o_ref[...] = z

    @pl.when(c < qk_chunks)
    def _():
        scale = jnp.where(c >= qk_chunks // 2, RET_DK ** -0.5, 1.0)
        for s0 in range(0, PROJ_CW, LANES):
            r = _rotary(z[:, s0:s0 + LANES], s0 % RET_DK, c_ref, sa_ref, sb_ref)
            o_ref[:, s0:s0 + LANES] = r * scale


def _ret_inproj_small(x, g, w, tabs):
    return pl.pallas_call(
        _ret_inproj_small_kernel,
        grid=(RET_IN // PROJ_CW,),
        in_specs=[
            _const_spec((SMALL_ROWS, D_MODEL)),
            _const_spec((None, 1, D_MODEL), (0, 0, 0)),
            pl.BlockSpec((None, D_MODEL, PROJ_CW), lambda c: (0, 0, c)),
            _const_spec((SMALL_ROWS, RET_DK)),
            _const_spec((SMALL_ROWS, RET_DK)),
            _const_spec((SMALL_ROWS, RET_DK)),
        ],
        out_specs=[
            pl.BlockSpec((SMALL_ROWS, PROJ_CW), lambda c: (0, c)),
            pl.BlockSpec((D_MODEL, PROJ_CW), lambda c: (0, c)),
        ],
        out_shape=[
            jax.ShapeDtypeStruct((SMALL_ROWS, RET_IN), F32),
            jax.ShapeDtypeStruct((D_MODEL, RET_IN), BF16),
        ],
        scratch_shapes=[pltpu.VMEM((SMALL_ROWS, D_MODEL), BF16)],
        compiler_params=_params(1),
        name="ret_inproj_small",
    )(x, g, w, *tabs)


def _gn_gate(o, g):
    mu = jnp.mean(o, axis=-1, keepdims=True)
    d = o - mu
    var = jnp.mean(d * d, axis=-1, keepdims=True)
    on = d * lax.rsqrt(var + GN_EPS)
    gf = g.astype(F32)
    return (gf / (1.0 + jnp.exp(-gf))) * on


def _lane_tile(x, n):
    return jnp.concatenate([x] * n, axis=1)


def _head_cols(h):
    q0 = h * RET_DK
    k0 = D_MODEL + h * RET_DK
    v0 = 2 * D_MODEL + h * RET_DV
    g0 = 2 * D_MODEL + RET_VDIM + h * RET_DV
    return (slice(q0, q0 + RET_DK), slice(k0, k0 + RET_DK),
            slice(v0, v0 + RET_DV), slice(g0, g0 + RET_DV))


def _decay_tables(decay_scr, cdec_scr, kdec_scr, chunk, valid):
    n_mat = lax.broadcasted_iota(jnp.int32, (chunk, chunk), 0)
    m_mat = lax.broadcasted_iota(jnp.int32, (chunk, chunk), 1)
    diff = (n_mat - m_mat).astype(F32)
    causal = diff >= 0
    diff = jnp.where(causal, diff, 0.0)
    n_col = lax.broadcasted_iota(jnp.int32, (chunk, LANES), 0).astype(F32)
    for h in range(RET_HEADS):
        lg = math.log(1.0 - 2.0 ** (-5.0 - h))
        decay_scr[h] = jnp.where(causal, jnp.exp(lg * diff), 0.0)
        cdec_scr[h] = jnp.exp(lg * (n_col + 1.0))
        kdec_scr[h] = jnp.where(n_col < valid, jnp.exp(lg * (valid - 1.0 - n_col)), 0.0)


def _ret_head(h, q, k, v, g, s_ref, decay_scr, cdec_scr, kdec_scr, gated_scr, valid):
    lg = math.log(1.0 - 2.0 ** (-5.0 - h))
    s_prev = s_ref[...]
    scores = lax.dot_general(q, k.astype(BF16), (((1,), (1,)), ((), ())),
                             preferred_element_type=F32)
    inner = jnp.dot((scores * decay_scr[h]).astype(BF16), v, preferred_element_type=F32)
    cross = jnp.dot(q, s_prev.astype(BF16), preferred_element_type=F32)
    o = inner + cross * _lane_tile(cdec_scr[h], RET_DV // LANES)
    kd = (k * _lane_tile(kdec_scr[h], RET_DK // LANES)).astype(BF16)
    kv = lax.dot_general(kd, v, (((0,), (0,)), ((), ())), preferred_element_type=F32)
    s_ref[...] = math.exp(lg * valid) * s_prev + kv
    gated_scr[:, h * RET_DV:(h + 1) * RET_DV] = _gn_gate(o, g).astype(BF16)


def _ret_meta_kernel(p_ref, x_ref, wo_ref, xo_ref, sf_ref, wob_ref,
                     gated_scr, decay_scr, cdec_scr, kdec_scr):
    _decay_tables(decay_scr, cdec_scr, kdec_scr, SMALL_BLOCK, N_META)
    sf_ref[...] = jnp.zeros_like(sf_ref)
    for h in range(RET_HEADS):
        qs, ks, vs, gs = _head_cols(h)
        _ret_head(h, p_ref[:, qs].astype(BF16), p_ref[:, ks], p_ref[:, vs].astype(BF16),
                  p_ref[:, gs], sf_ref.at[h], decay_scr, cdec_scr, kdec_scr, gated_scr, N_META)
    wo = wo_ref[...].astype(BF16)
    wob_ref[...] = wo
    xo_ref[...] = x_ref[...] + jnp.dot(gated_scr[...], wo, preferred_element_type=F32)


def _ret_meta(proj_small, x_small, w_out):
    nb = SMALL_BLOCK
    state_shape = (RET_HEADS, RET_DK, RET_DV)
    w_shape = (RET_VDIM, D_MODEL)
    return pl.pallas_call(
        _ret_meta_kernel,
        grid=(1,),
        in_specs=[
            _const_spec((nb, RET_IN)),
            _const_spec((nb, D_MODEL)),
            _const_spec((None,) + w_shape, (0, 0, 0)),
        ],
        out_specs=[
            pl.BlockSpec((nb, D_MODEL), lambda i: (0, 0)),
            pl.BlockSpec(state_shape, lambda i: (0, 0, 0)),
            pl.BlockSpec(w_shape, lambda i: (0, 0)),
        ],
        out_shape=[
            jax.ShapeDtypeStruct((nb, D_MODEL), F32),
            jax.ShapeDtypeStruct(state_shape, F32),
            jax.ShapeDtypeStruct(w_shape, BF16),
        ],
        scratch_shapes=[
            pltpu.VMEM((nb, RET_VDIM), BF16),
            pltpu.VMEM((RET_HEADS, nb, nb), F32),
            pltpu.VMEM((RET_HEADS, nb, LANES), F32),
            pltpu.VMEM((RET_HEADS, nb, LANES), F32),
        ],
        compiler_params=_params(1),
        name="ret_meta",
    )(proj_small, x_small, w_out)


def _ret_layer_kernel(x_ref, g_ref, w_ref, c_ref, sa_ref, sb_ref, s0_ref, wo_ref, ps_ref, st_ref,
                      xo_ref, sf_ref, os_ref, sn_ref,
                      gated_scr, decay_scr, cdec_scr, kdec_scr, *, chunk, bb):
    b = pl.program_id(0)
    c = pl.program_id(1)

    @pl.when((b == 0) & (c == 0))
    def _():
        _decay_tables(decay_scr, cdec_scr, kdec_scr, chunk, chunk)

    @pl.when(c == 0)
    def _():
        sf_ref[0] = s0_ref[...]

    step = b * pl.num_programs(1) + c
    _sample_state_update(ps_ref, st_ref, os_ref, sn_ref, step * bb, bb)

    hn = _rms(x_ref[...], g_ref[...]).astype(BF16)
    proj = lambda sl: jnp.dot(hn, w_ref[:, sl], preferred_element_type=F32)
    rot = lambda z: jnp.concatenate(
        [_rotary(z[:, s0:s0 + LANES], s0, c_ref, sa_ref, sb_ref)
         for s0 in range(0, RET_DK, LANES)], axis=1)
    for h in range(RET_HEADS):
        qs, ks, vs, gs = _head_cols(h)
        q = rot(proj(qs)).astype(BF16)
        k = rot(proj(ks)) * (RET_DK ** -0.5)
        _ret_head(h, q, k, proj(vs).astype(BF16), proj(gs), sf_ref.at[0, h],
                  decay_scr, cdec_scr, kdec_scr, gated_scr, chunk)
    xo_ref[...] = x_ref[...] + jnp.dot(gated_scr[...], wo_ref[...],
                                       preferred_element_type=F32)


def _ret_layer(x, g, w_in_b, tabs, s0, w_out_b, proj_small, state, *, chunk):
    n_chunks = SEQ // chunk
    steps = BATCH * n_chunks
    bb = DEC_BATCH // steps
    row_map = lambda b, c: (b * n_chunks + c, 0)
    state_shape = (RET_HEADS, RET_DK, RET_DV)
    tab_spec = pl.BlockSpec((chunk, RET_DK), lambda b, c: (c, 0))
    samp_spec = pl.BlockSpec((None, bb) + state_shape,
                             lambda b, c: (0, b * n_chunks + c, 0, 0, 0))
    return pl.pallas_call(
        functools.partial(_ret_layer_kernel, chunk=chunk, bb=bb),
        grid=(BATCH, n_chunks),
        in_specs=[
            pl.BlockSpec((chunk, D_MODEL), row_map),
            _const_spec((None, 1, D_MODEL), (0, 0, 0)),
            _const_spec((D_MODEL, RET_IN)),
            tab_spec, tab_spec, tab_spec,
            _const_spec(state_shape),
            _const_spec((RET_VDIM, D_MODEL)),
            _const_spec((SMALL_BLOCK, RET_IN), (1, 0)),
            samp_spec,
        ],
        out_specs=[
            pl.BlockSpec((chunk, D_MODEL), row_map),
            pl.BlockSpec((1,) + state_shape, lambda b, c: (b, 0, 0, 0)),
            pl.BlockSpec((DEC_BATCH, RET_VDIM), lambda b, c: (0, 0)),
            samp_spec,
        ],
        out_shape=[
            jax.ShapeDtypeStruct(x.shape, F32),
            jax.ShapeDtypeStruct((BATCH,) + state_shape, F32),
            jax.ShapeDtypeStruct((DEC_BATCH, RET_VDIM), F32),
            jax.ShapeDtypeStruct(state.shape, F32),
        ],
        scratch_shapes=[
            pltpu.VMEM((chunk, RET_VDIM), BF16),
            pltpu.VMEM((RET_HEADS, chunk, chunk), F32),
            pltpu.VMEM((RET_HEADS, chunk, LANES), F32),
            pltpu.VMEM((RET_HEADS, chunk, LANES), F32),
        ],
        compiler_params=pltpu.CompilerParams(
            dimension_semantics=("arbitrary", "arbitrary"),
            vmem_limit_bytes=RET_LAYER_VMEM_LIMIT),
        name="ret_layer",
    )(x, g, w_in_b, *tabs, s0, w_out_b, proj_small, state)


def _gn_outproj_kernel(o_ref, g_ref, x_ref, wo_ref, xo_ref):
    parts = []
    for h in range(RET_HEADS):
        sl = slice(h * RET_DV, (h + 1) * RET_DV)
        parts.append(_gn_gate(o_ref[:, sl], g_ref[:, sl]).astype(BF16))
    gated = jnp.concatenate(parts, axis=1)
    xo_ref[...] = x_ref[...] + jnp.dot(gated, wo_ref[...], preferred_element_type=F32)


def _gn_outproj(o, proj_small, x_small, w_out_b):
    return pl.pallas_call(
        _gn_outproj_kernel,
        grid=(1,),
        in_specs=[
            _const_spec((DEC_BATCH, RET_VDIM)),
            _const_spec((SMALL_BLOCK, RET_VDIM), (1, 2)),
            _const_spec((SMALL_BLOCK, D_MODEL), (1, 0)),
            _const_spec((RET_VDIM, D_MODEL)),
        ],
        out_specs=pl.BlockSpec((DEC_BATCH, D_MODEL), lambda i: (0, 0)),
        out_shape=jax.ShapeDtypeStruct((DEC_BATCH, D_MODEL), F32),
        compiler_params=_params(1),
        name="gn_outproj",
    )(o, proj_small, x_small, w_out_b)


def _n_parts(kind):
    return 2 if kind == "ffn" else 3


def _mlp_pre(parts, kind):
    if kind == "ffn":
        return parts[0], parts[1]
    return parts[1] * parts[2], parts[0]


def _mlp_act(pre, p1, p2, other, cw, cb, kind):
    conv = cw[0:1] * p2 + cw[1:2] * p1 + cw[2:3] * pre
    if kind == "ffn":
        a = conv + cb
        return (a / (1.0 + jnp.exp(-a))) * other
    return other * conv


def _mlp_seq_kernel(*refs, tm, dff, kind, final_norm):
    npart = _n_parts(kind)
    x_ref, g_ref = refs[0:2]
    w_refs = refs[2:2 + npart]
    (cw_ref, cb_ref, wo_ref, carry0_ref, gf_ref,
     xo_ref, st_ref, carry_scr, pre_scr, act_scr) = refs[2 + npart:]

    @pl.when(pl.program_id(1) == 0)
    def _():
        carry_scr[...] = carry0_ref[...]

    h = _rms(x_ref[...], g_ref[...]).astype(BF16)
    for c0 in range(0, dff, MLP_CW):
        sl = slice(c0, c0 + MLP_CW)
        parts = [jnp.dot(h, w[:, sl], preferred_element_type=F32) for w in w_refs]
        pre, other = _mlp_pre(parts, kind)
        pre_scr[0:SUBLANES, :] = carry_scr[:, sl]
        pre_scr[SUBLANES:SUBLANES + tm, :] = pre
        p1 = pre_scr[SUBLANES - 1:SUBLANES - 1 + tm, :]
        p2 = pre_scr[SUBLANES - 2:SUBLANES - 2 + tm, :]
        act_scr[:, sl] = _mlp_act(pre, p1, p2, other, cw_ref[:, sl], cb_ref[:, sl],
                                  kind).astype(BF16)
        carry_scr[:, sl] = pre[tm - SUBLANES:tm, :]
    out = x_ref[...] + jnp.dot(act_scr[...], wo_ref[...], preferred_element_type=F32)
    if final_norm:
        out = _rms(out, gf_ref[...])
    xo_ref[...] = out
    st_ref[0] = carry_scr[SUBLANES - 2:SUBLANES, :]


def _mlp_seq(x, g, g_layer, w_parts, conv_w, conv_b, w_layer, w_out_b, pre_small, g_final,
             *, kind, final_norm, tm=512):
    dff = w_out_b.shape[0]
    tiles = SEQ // tm
    row_map = lambda b, t: (b * tiles + t, 0)
    return pl.pallas_call(
        functools.partial(_mlp_seq_kernel, tm=tm, dff=dff, kind=kind, final_norm=final_norm),
        grid=(BATCH, tiles),
        in_specs=[
            pl.BlockSpec((tm, D_MODEL), row_map),
            _const_spec((None, 1, D_MODEL), (g_layer, 0, 0)),
            *[_const_spec((D_MODEL, dff)) for _ in w_parts],
            _const_spec((None, 3, dff), (w_layer, 0, 0)),
            _const_spec((None, 1, dff), (w_layer, 0, 0)),
            _const_spec((dff, D_MODEL)),
            _const_spec((SUBLANES, dff), (N_META // SUBLANES - 1, 0)),
            _const_spec((None, 1, D_MODEL), (0, 0, 0)),
        ],
        out_specs=[
            pl.BlockSpec((tm, D_MODEL), row_map),
            pl.BlockSpec((1, 2, dff), lambda b, t: (b, 0, 0)),
        ],
        out_shape=[
            jax.ShapeDtypeStruct(x.shape, F32),
            jax.ShapeDtypeStruct((BATCH, 2, dff), F32),
        ],
        scratch_shapes=[
            pltpu.VMEM((SUBLANES, dff), F32),
            pltpu.VMEM((tm + SUBLANES, MLP_CW), F32),
            pltpu.VMEM((tm, dff), BF16),
        ],
        compiler_params=_params(2),
        name="mlp_seq_" + kind,
    )(x, g, *w_parts, conv_w, conv_b, w_out_b, pre_small, g_final)


def _mlp_small_kernel(*refs, kind, final_norm):
    npart = _n_parts(kind)
    nb = SMALL_BLOCK
    x_ref, g_ref = refs[0:2]
    w_refs = refs[2:2 + npart]
    (cw_ref, cb_ref, wo_ref, s0_ref, s1_ref, gf_ref) = refs[2 + npart:8 + npart]
    xo_ref, pre_ref = refs[8 + npart:10 + npart]
    wb_refs = refs[10 + npart:10 + 2 * npart]
    wob_ref, h_scr, acc_scr, pre_scr = refs[10 + 2 * npart:]
    c = pl.program_id(0)

    @pl.when(c == 0)
    def _():
        h_scr[...] = _rms(x_ref[...], g_ref[...]).astype(BF16)
        acc_scr[...] = jnp.zeros_like(acc_scr)

    parts = []
    for w_ref, wb_ref in zip(w_refs, wb_refs):
        wb = w_ref[...].astype(BF16)
        wb_ref[...] = wb
        parts.append(jnp.dot(h_scr[...], wb, preferred_element_type=F32))
    pre, other = _mlp_pre(parts, kind)
    pre_ref[...] = pre
    cw = cw_ref[...]
    cb = cb_ref[...]
    pre_scr[0:SUBLANES, :] = jnp.zeros((SUBLANES, MLP_CW), F32)
    pre_scr[SUBLANES:SUBLANES + nb, :] = pre[0:nb]
    act_meta = _mlp_act(pre[0:nb], pre_scr[SUBLANES - 1:SUBLANES - 1 + nb, :],
                        pre_scr[SUBLANES - 2:SUBLANES - 2 + nb, :], other[0:nb], cw, cb, kind)
    act_samp = _mlp_act(pre[nb:], s1_ref[...], s0_ref[...], other[nb:], cw, cb, kind)
    act = jnp.concatenate([act_meta, act_samp], axis=0).astype(BF16)
    wob = wo_ref[...].astype(BF16)
    wob_ref[...] = wob
    acc_scr[...] += jnp.dot(act, wob, preferred_element_type=F32)

    @pl.when(c == pl.num_programs(0) - 1)
    def _():
        out = x_ref[...] + acc_scr[...]
        if final_norm:
            out = _rms(out, gf_ref[...])
        xo_ref[...] = out


def _mlp_small(x, g, g_layer, w_in, conv_w, conv_b, w_out, state2, layer, g_final,
               *, kind, final_norm):
    npart = _n_parts(kind)
    dff = w_out.shape[1]
    nc = dff // MLP_CW
    chunk_map = lambda c: (0, c)
    return pl.pallas_call(
        functools.partial(_mlp_small_kernel, kind=kind, final_norm=final_norm),
        grid=(nc,),
        in_specs=[
            _const_spec((SMALL_ROWS, D_MODEL)),
            _const_spec((None, 1, D_MODEL), (g_layer, 0, 0)),
            *[pl.BlockSpec((None, D_MODEL, MLP_CW), functools.partial(
                lambda c, p: (layer, 0, p * nc + c), p=p)) for p in range(npart)],
            pl.BlockSpec((None, 3, MLP_CW), lambda c: (layer, 0, c)),
            pl.BlockSpec((None, 1, MLP_CW), lambda c: (layer, 0, c)),
            pl.BlockSpec((None, MLP_CW, D_MODEL), lambda c: (layer, c, 0)),
            pl.BlockSpec((None, DEC_BATCH, MLP_CW), lambda c: (layer, 0, c)),
            pl.BlockSpec((None, DEC_BATCH, MLP_CW), lambda c: (layer, 0, nc + c)),
            _const_spec((None, 1, D_MODEL), (0, 0, 0)),
        ],
        out_specs=[
            pl.BlockSpec((SMALL_ROWS, D_MODEL), lambda c: (0, 0)),
            pl.BlockSpec((SMALL_ROWS, MLP_CW), chunk_map),
            *[pl.BlockSpec((D_MODEL, MLP_CW), chunk_map) for _ in range(npart)],
            pl.BlockSpec((MLP_CW, D_MODEL), lambda c: (c, 0)),
        ],
        out_shape=[
            jax.ShapeDtypeStruct((SMALL_ROWS, D_MODEL), F32),
            jax.ShapeDtypeStruct((SMALL_ROWS, dff), F32),
            *[jax.ShapeDtypeStruct((D_MODEL, dff), BF16) for _ in range(npart)],
            jax.ShapeDtypeStruct((dff, D_MODEL), BF16),
        ],
        scratch_shapes=[
            pltpu.VMEM((SMALL_ROWS, D_MODEL), BF16),
            pltpu.VMEM((SMALL_ROWS, D_MODEL), F32),
            pltpu.VMEM((SMALL_BLOCK + SUBLANES, MLP_CW), F32),
        ],
        compiler_params=_params(1),
        name="mlp_small_" + kind,
    )(x, g, *([w_in] * npart), conv_w, conv_b, w_out, state2, state2, g_final)


def _rope_tables(pos):
    half = RET_DK // 2
    inv = 1.0 / (ROPE_BASE ** jnp.linspace(0.0, 1.0, half, dtype=F32))
    ang = pos.astype(F32)[:, None] * inv[None, :]
    cos = jnp.repeat(jnp.cos(ang), 2, axis=1)
    sin = jnp.repeat(jnp.sin(ang), 2, axis=1)
    even = (jnp.arange(RET_DK) % 2 == 0)[None, :]
    sa = jnp.where(even, -sin, 0.0)
    sb = jnp.where(even, 0.0, sin)
    return cos, sa, sb


def kernel(x_prompt, x_sample, state_ret, state_conv, state_ffn, meta_tokens, norm_mix, norm_ffn,
           norm_final, w_ret_in, w_ret_out, w_sc_in, w_sc_conv, w_sc_out, w_ffn_in, w_ffn_conv,
           b_ffn_conv, w_ffn_out):
    nb = SMALL_BLOCK
    xb = x_prompt.reshape(BATCH * SEQ, D_MODEL)
    xs = jnp.concatenate([meta_tokens, jnp.zeros((nb - N_META, D_MODEL), F32),
                          x_sample.reshape(DEC_BATCH, D_MODEL)], axis=0)

    g_mix = norm_mix.reshape(2, 1, D_MODEL)
    g_ffn = norm_ffn.reshape(2, 1, D_MODEL)
    g_final = norm_final.reshape(1, 1, D_MODEL)
    b_ffn = b_ffn_conv.reshape(2, 1, D_FF)
    st_ffn = state_ffn.reshape(2, DEC_BATCH, 2 * D_FF)
    st_conv = state_conv.reshape(1, DEC_BATCH, 2 * D_MODEL)

    pos_big = N_META + jnp.arange(SEQ, dtype=jnp.int32)
    pos_small = jnp.concatenate([jnp.arange(nb, dtype=jnp.int32),
                                 jnp.full((nb,), PAST_LEN, jnp.int32)])
    tabs_big = _rope_tables(pos_big)
    tabs_small = _rope_tables(pos_small)

    proj_s, w_ret_in_b = _ret_inproj_small(xs, g_mix, w_ret_in, tabs_small)
    xs_meta, s_meta, w_ret_out_b = _ret_meta(proj_s, xs, w_ret_out)
    xb, ret_p, o_s, ret_s = _ret_layer(xb, g_mix, w_ret_in_b, tabs_big, s_meta, w_ret_out_b,
                                       proj_s, state_ret, chunk=256)
    xs_samp = _gn_outproj(o_s, proj_s, xs, w_ret_out_b)
    xs = jnp.concatenate([xs_meta, xs_samp], axis=0)

    def ffn(i, xs, xb, final_norm):
        xs, pre_s, w_up_b, w_gate_b, w_out_b = _mlp_small(
            xs, g_ffn, i, w_ffn_in, w_ffn_conv, b_ffn, w_ffn_out, st_ffn, i, g_final,
            kind="ffn", final_norm=final_norm)
        xb, st_p = _mlp_seq(xb, g_ffn, i, (w_up_b, w_gate_b), w_ffn_conv, b_ffn, i, w_out_b,
                            pre_s, g_final, kind="ffn", final_norm=final_norm)
        st_s = jnp.stack([state_ffn[i, :, 1], pre_s[nb:]], axis=1)
        return xs, xb, st_p, st_s

    xs, xb, ffn_p0, ffn_s0 = ffn(0, xs, xb, False)

    zero_b = jnp.zeros((1, 1, D_MODEL), F32)
    xs, pre_s, w_bg_b, w_cg_b, w_xin_b, w_sc_out_b = _mlp_small(
        xs, g_mix, 1, w_sc_in, w_sc_conv, zero_b, w_sc_out, st_conv, 0, g_final,
        kind="sc", final_norm=False)
    xb, conv_p = _mlp_seq(xb, g_mix, 1, (w_bg_b, w_cg_b, w_xin_b), w_sc_conv, zero_b, 0,
                          w_sc_out_b, pre_s, g_final, kind="sc", final_norm=False)
    conv_s = jnp.stack([state_conv[0, :, 1], pre_s[nb:]], axis=1)

    xs, xb, ffn_p1, ffn_s1 = ffn(1, xs, xb, True)

    y_prompt = xb.reshape(BATCH, SEQ, D_MODEL)
    y_sample = xs[nb:].reshape(DEC_BATCH, 1, D_MODEL)
    return (y_prompt, y_sample, ret_p[None], ret_s, conv_p[None], conv_s[None],
            jnp.stack([ffn_p0, ffn_p1]), jnp.stack([ffn_s0, ffn_s1]))
```

```python
import functools
import math

import jax
import jax.numpy as jnp
from jax import lax
from jax.experimental import pallas as pl
from jax.experimental.pallas import tpu as pltpu

F32 = jnp.float32
BF16 = jnp.bfloat16

D_MODEL = 1024
SEQ = 2048
BATCH = 8
DEC_BATCH = 128
PAST_LEN = 16384
N_META = 16
RET_HEADS = 4
RET_DK = 256
RET_DV = 512
RET_VDIM = RET_HEADS * RET_DV
RET_IN = 2 * D_MODEL + 2 * RET_VDIM
ROPE_BASE = 10000.0
D_FF = 2816
NORM_EPS = 1e-6
GN_EPS = 1e-6

LANES = 128
SUBLANES = 8
SMALL_BLOCK = 128
SMALL_ROWS = 2 * SMALL_BLOCK
MLP_CW = 256
PROJ_CW = 512
VMEM_LIMIT = 56 * 1024 * 1024
RET_LAYER_VMEM_LIMIT = 60 * 1024 * 1024


def _params(n_axes):
    return pltpu.CompilerParams(
        dimension_semantics=("arbitrary",) * n_axes,
        vmem_limit_bytes=VMEM_LIMIT)


def _const_spec(shape, index=None):
    index = (0,) * len(shape) if index is None else index
    return pl.BlockSpec(shape, lambda *_: index, pipeline_mode=pl.Buffered(1))


def _rms(x, g):
    ms = jnp.mean(x * x, axis=-1, keepdims=True)
    return x * lax.rsqrt(ms + NORM_EPS) * g


def _rotary(zs, col, c_ref, sa_ref, sb_ref):
    sl = slice(col, col + LANES)
    return (zs * c_ref[:, sl] + pltpu.roll(zs, LANES - 1, 1) * sa_ref[:, sl]
            + pltpu.roll(zs, 1, 1) * sb_ref[:, sl])


def _sample_state_update(p_ref, qkt_ref, s_ref, o_ref, sn_ref, first, count):
    tile = RET_DV // LANES
    for j in range(count):
        b = first + j
        row = p_ref[pl.ds(b, 1), :]
        lane_b = jnp.full((RET_DK, LANES), b, jnp.int32)
        for h in range(RET_HEADS):
            gamma = 1.0 - 2.0 ** (-5.0 - h)
            qs, ks, vs, _ = _head_cols(h)
            v = row[:, vs]
            q_rows = jnp.take_along_axis(qkt_ref[qs, :], lane_b, axis=1)
            k_rows = jnp.take_along_axis(qkt_ref[ks, :], lane_b, axis=1)
            s_prev = s_ref[j, h]
            cross = jnp.sum(s_prev * _lane_tile(q_rows, tile), axis=0, keepdims=True)
            qk = jnp.sum(row[:, qs] * row[:, ks], axis=-1, keepdims=True)
            o_ref[pl.ds(b, 1), h * RET_DV:(h + 1) * RET_DV] = qk * v + gamma * cross
            sn_ref[j, h] = gamma * s_prev + _lane_tile(k_rows, tile) * v


QK_CHUNKS = 2 * D_MODEL // PROJ_CW


def _ret_inproj_small_kernel(x_ref, g_ref, w_ref, c_ref, sa_ref, sb_ref,
                             o_ref, wb_ref, qkt_ref, h_scr):
    c = pl.program_id(0)

    @pl.when(c == 0)
    def _():
        h_scr[...] = _rms(x_ref[...], g_ref[...]).astype(BF16)

    wb = w_ref[...].astype(BF16)
    wb_ref[...] = wb
    z = jnp.dot(h_scr[...], wb, preferred_element_type=F32)

    @pl.when(c >= QK_CHUNKS)
    def _():
        o_ref[...] = z

    @pl.when(c < QK_CHUNKS)
    def _():
        scale = jnp.where(c >= QK_CHUNKS // 2, RET_DK ** -0.5, 1.0)
        for s0 in range(0, PROJ_CW, LANES):
            r = _rotary(z[:, s0:s0 + LANES], s0 % RET_DK, c_ref, sa_ref, sb_ref) * scale
            o_ref[:, s0:s0 + LANES] = r
            qkt_ref[s0:s0 + LANES, :] = r[SMALL_BLOCK:].T


def _ret_inproj_small(x, g, w, tabs):
    return pl.pallas_call(
        _ret_inproj_small_kernel,
        grid=(RET_IN // PROJ_CW,),
        in_specs=[
            _const_spec((SMALL_ROWS, D_MODEL)),
            _const_spec((None, 1, D_MODEL), (0, 0, 0)),
            pl.BlockSpec((None, D_MODEL, PROJ_CW), lambda c: (0, 0, c)),
            _const_spec((SMALL_ROWS, RET_DK)),
            _const_spec((SMALL_ROWS, RET_DK)),
            _const_spec((SMALL_ROWS, RET_DK)),
        ],
        out_specs=[
            pl.BlockSpec((SMALL_ROWS, PROJ_CW), lambda c: (0, c)),
            pl.BlockSpec((D_MODEL, PROJ_CW), lambda c: (0, c)),
            pl.BlockSpec((PROJ_CW, DEC_BATCH), lambda c: (jnp.minimum(c, QK_CHUNKS - 1), 0)),
        ],
        out_shape=[
            jax.ShapeDtypeStruct((SMALL_ROWS, RET_IN), F32),
            jax.ShapeDtypeStruct((D_MODEL, RET_IN), BF16),
            jax.ShapeDtypeStruct((2 * D_MODEL, DEC_BATCH), F32),
        ],
        scratch_shapes=[pltpu.VMEM((SMALL_ROWS, D_MODEL), BF16)],
        compiler_params=_params(1),
        name="ret_inproj_small",
    )(x, g, w, *tabs)


def _gn_gate(o, g):
    mu = jnp.mean(o, axis=-1, keepdims=True)
    d = o - mu
    var = jnp.mean(d * d, axis=-1, keepdims=True)
    on = d * lax.rsqrt(var + GN_EPS)
    gf = g.astype(F32)
    return (gf / (1.0 + jnp.exp(-gf))) * on


def _lane_tile(x, n):
    return jnp.concatenate([x] * n, axis=1)


def _head_cols(h):
    q0 = h * RET_DK
    k0 = D_MODEL + h * RET_DK
    v0 = 2 * D_MODEL + h * RET_DV
    g0 = 2 * D_MODEL + RET_VDIM + h * RET_DV
    return (slice(q0, q0 + RET_DK), slice(k0, k0 + RET_DK),
            slice(v0, v0 + RET_DV), slice(g0, g0 + RET_DV))


def _decay_tables(decay_scr, cdec_scr, kdec_scr, chunk, valid):
    n_mat = lax.broadcasted_iota(jnp.int32, (chunk, chunk), 0)
    m_mat = lax.broadcasted_iota(jnp.int32, (chunk, chunk), 1)
    diff = (n_mat - m_mat).astype(F32)
    causal = diff >= 0
    diff = jnp.where(causal, diff, 0.0)
    n_col = lax.broadcasted_iota(jnp.int32, (chunk, LANES), 0).astype(F32)
    for h in range(RET_HEADS):
        lg = math.log(1.0 - 2.0 ** (-5.0 - h))
        decay_scr[h] = jnp.where(causal, jnp.exp(lg * diff), 0.0)
        cdec_scr[h] = jnp.exp(lg * (n_col + 1.0))
        kdec_scr[h] = jnp.where(n_col < valid, jnp.exp(lg * (valid - 1.0 - n_col)), 0.0)


def _ret_head(h, q, k, v, g, s_ref, decay_scr, cdec_scr, kdec_scr, gated_scr, valid):
    lg = math.log(1.0 - 2.0 ** (-5.0 - h))
    s_prev = s_ref[...]
    scores = lax.dot_general(q, k.astype(BF16), (((1,), (1,)), ((), ())),
                             preferred_element_type=F32)
    inner = jnp.dot((scores * decay_scr[h]).astype(BF16), v, preferred_element_type=F32)
    cross = jnp.dot(q, s_prev.astype(BF16), preferred_element_type=F32)
    o = inner + cross * _lane_tile(cdec_scr[h], RET_DV // LANES)
    kd = (k * _lane_tile(kdec_scr[h], RET_DK // LANES)).astype(BF16)
    kv = lax.dot_general(kd, v, (((0,), (0,)), ((), ())), preferred_element_type=F32)
    s_ref[...] = math.exp(lg * valid) * s_prev + kv
    gated_scr[:, h * RET_DV:(h + 1) * RET_DV] = _gn_gate(o, g).astype(BF16)


def _ret_meta_kernel(p_ref, x_ref, wo_ref, xo_ref, sf_ref, wob_ref,
                     gated_scr, decay_scr, cdec_scr, kdec_scr):
    _decay_tables(decay_scr, cdec_scr, kdec_scr, SMALL_BLOCK, N_META)
    sf_ref[...] = jnp.zeros_like(sf_ref)
    for h in range(RET_HEADS):
        qs, ks, vs, gs = _head_cols(h)
        _ret_head(h, p_ref[:, qs].astype(BF16), p_ref[:, ks], p_ref[:, vs].astype(BF16),
                  p_ref[:, gs], sf_ref.at[h], decay_scr, cdec_scr, kdec_scr, gated_scr, N_META)
    wo = wo_ref[...].astype(BF16)
    wob_ref[...] = wo
    xo_ref[...] = x_ref[...] + jnp.dot(gated_scr[...], wo, preferred_element_type=F32)


def _ret_meta(proj_small, x_small, w_out):
    nb = SMALL_BLOCK
    state_shape = (RET_HEADS, RET_DK, RET_DV)
    w_shape = (RET_VDIM, D_MODEL)
    return pl.pallas_call(
        _ret_meta_kernel,
        grid=(1,),
        in_specs=[
            _const_spec((nb, RET_IN)),
            _const_spec((nb, D_MODEL)),
            _const_spec((None,) + w_shape, (0, 0, 0)),
        ],
        out_specs=[
            pl.BlockSpec((nb, D_MODEL), lambda i: (0, 0)),
            pl.BlockSpec(state_shape, lambda i: (0, 0, 0)),
            pl.BlockSpec(w_shape, lambda i: (0, 0)),
        ],
        out_shape=[
            jax.ShapeDtypeStruct((nb, D_MODEL), F32),
            jax.ShapeDtypeStruct(state_shape, F32),
            jax.ShapeDtypeStruct(w_shape, BF16),
        ],
        scratch_shapes=[
            pltpu.VMEM((nb, RET_VDIM), BF16),
            pltpu.VMEM((RET_HEADS, nb, nb), F32),
            pltpu.VMEM((RET_HEADS, nb, LANES), F32),
            pltpu.VMEM((RET_HEADS, nb, LANES), F32),
        ],
        compiler_params=_params(1),
        name="ret_meta",
    )(proj_small, x_small, w_out)


def _ret_layer_kernel(x_ref, g_ref, w_ref, c_ref, sa_ref, sb_ref, s0_ref, wo_ref, ps_ref,
                      qkt_ref, st_ref, xo_ref, sf_ref, os_ref, sn_ref,
                      gated_scr, decay_scr, cdec_scr, kdec_scr, *, chunk, bb):
    b = pl.program_id(0)
    c = pl.program_id(1)

    @pl.when((b == 0) & (c == 0))
    def _():
        _decay_tables(decay_scr, cdec_scr, kdec_scr, chunk, chunk)

    @pl.when(c == 0)
    def _():
        sf_ref[0] = s0_ref[...]

    step = b * pl.num_programs(1) + c
    _sample_state_update(ps_ref, qkt_ref, st_ref, os_ref, sn_ref, step * bb, bb)

    hn = _rms(x_ref[...], g_ref[...]).astype(BF16)
    proj = lambda sl: jnp.dot(hn, w_ref[:, sl], preferred_element_type=F32)
    rot = lambda z: jnp.concatenate(
        [_rotary(z[:, s0:s0 + LANES], s0, c_ref, sa_ref, sb_ref)
         for s0 in range(0, RET_DK, LANES)], axis=1)
    for h in range(RET_HEADS):
        qs, ks, vs, gs = _head_cols(h)
        q = rot(proj(qs)).astype(BF16)
        k = rot(proj(ks)) * (RET_DK ** -0.5)
        _ret_head(h, q, k, proj(vs).astype(BF16), proj(gs), sf_ref.at[0, h],
                  decay_scr, cdec_scr, kdec_scr, gated_scr, chunk)
    xo_ref[...] = x_ref[...] + jnp.dot(gated_scr[...], wo_ref[...],
                                       preferred_element_type=F32)


def _ret_layer(x, g, w_in_b, tabs, s0, w_out_b, proj_small, qkt_small, state, *, chunk):
    n_chunks = SEQ // chunk
    steps = BATCH * n_chunks
    bb = DEC_BATCH // steps
    row_map = lambda b, c: (b * n_chunks + c, 0)
    state_shape = (RET_HEADS, RET_DK, RET_DV)
    tab_spec = pl.BlockSpec((chunk, RET_DK), lambda b, c: (c, 0))
    samp_spec = pl.BlockSpec((None, bb) + state_shape,
                             lambda b, c: (0, b * n_chunks + c, 0, 0, 0))
    return pl.pallas_call(
        functools.partial(_ret_layer_kernel, chunk=chunk, bb=bb),
        grid=(BATCH, n_chunks),
        in_specs=[
            pl.BlockSpec((chunk, D_MODEL), row_map),
            _const_spec((None, 1, D_MODEL), (0, 0, 0)),
            _const_spec((D_MODEL, RET_IN)),
            tab_spec, tab_spec, tab_spec,
            _const_spec(state_shape),
            _const_spec((RET_VDIM, D_MODEL)),
            _const_spec((SMALL_BLOCK, RET_IN), (1, 0)),
            _const_spec((2 * D_MODEL, DEC_BATCH)),
            samp_spec,
        ],
        out_specs=[
            pl.BlockSpec((chunk, D_MODEL), row_map),
            pl.BlockSpec((1,) + state_shape, lambda b, c: (b, 0, 0, 0)),
            pl.BlockSpec((DEC_BATCH, RET_VDIM), lambda b, c: (0, 0)),
            samp_spec,
        ],
        out_shape=[
            jax.ShapeDtypeStruct(x.shape, F32),
            jax.ShapeDtypeStruct((BATCH,) + state_shape, F32),
            jax.ShapeDtypeStruct((DEC_BATCH, RET_VDIM), F32),
            jax.ShapeDtypeStruct(state.shape, F32),
        ],
        scratch_shapes=[
            pltpu.VMEM((chunk, RET_VDIM), BF16),
            pltpu.VMEM((RET_HEADS, chunk, chunk), F32),
            pltpu.VMEM((RET_HEADS, chunk, LANES), F32),
            pltpu.VMEM((RET_HEADS, chunk, LANES), F32),
        ],
        compiler_params=pltpu.CompilerParams(
            dimension_semantics=("arbitrary", "arbitrary"),
            vmem_limit_bytes=RET_LAYER_VMEM_LIMIT),
        name="ret_layer",
    )(x, g, w_in_b, *tabs, s0, w_out_b, proj_small, qkt_small, state)


def _gn_outproj_kernel(o_ref, g_ref, x_ref, wo_ref, xo_ref):
    parts = []
    for h in range(RET_HEADS):
        sl = slice(h * RET_DV, (h + 1) * RET_DV)
        parts.append(_gn_gate(o_ref[:, sl], g_ref[:, sl]).astype(BF16))
    gated = jnp.concatenate(parts, axis=1)
    xo_ref[...] = x_ref[...] + jnp.dot(gated, wo_ref[...], preferred_element_type=F32)


def _gn_outproj(o, proj_small, x_small, w_out_b):
    return pl.pallas_call(
        _gn_outproj_kernel,
        grid=(1,),
        in_specs=[
            _const_spec((DEC_BATCH, RET_VDIM)),
            _const_spec((SMALL_BLOCK, RET_VDIM), (1, 2)),
            _const_spec((SMALL_BLOCK, D_MODEL), (1, 0)),
            _const_spec((RET_VDIM, D_MODEL)),
        ],
        out_specs=pl.BlockSpec((DEC_BATCH, D_MODEL), lambda i: (0, 0)),
        out_shape=jax.ShapeDtypeStruct((DEC_BATCH, D_MODEL), F32),
        compiler_params=_params(1),
        name="gn_outproj",
    )(o, proj_small, x_small, w_out_b)


def _n_parts(kind):
    return 2 if kind == "ffn" else 3


def _mlp_pre(parts, kind):
    if kind == "ffn":
        return parts[0], parts[1]
    return parts[1] * parts[2], parts[0]


def _mlp_act(pre, p1, p2, other, cw, cb, kind):
    conv = cw[0:1] * p2 + cw[1:2] * p1 + cw[2:3] * pre
    if kind == "ffn":
        a = conv + cb
        return (a / (1.0 + jnp.exp(-a))) * other
    return other * conv


def _mlp_seq_kernel(*refs, tm, dff, kind, final_norm):
    npart = _n_parts(kind)
    x_ref, g_ref = refs[0:2]
    w_refs = refs[2:2 + npart]
    (cw_ref, cb_ref, wo_ref, carry0_ref, gf_ref,
     xo_ref, st_ref, carry_scr, pre_scr, act_scr) = refs[2 + npart:]

    @pl.when(pl.program_id(1) == 0)
    def _():
        carry_scr[...] = carry0_ref[...]

    h = _rms(x_ref[...], g_ref[...]).astype(BF16)
    for c0 in range(0, dff, MLP_CW):
        sl = slice(c0, c0 + MLP_CW)
        parts = [jnp.dot(h, w[:, sl], preferred_element_type=F32) for w in w_refs]
        pre, other = _mlp_pre(parts, kind)
        pre_scr[0:SUBLANES, :] = carry_scr[:, sl]
        pre_scr[SUBLANES:SUBLANES + tm, :] = pre
        p1 = pre_scr[SUBLANES - 1:SUBLANES - 1 + tm, :]
        p2 = pre_scr[SUBLANES - 2:SUBLANES - 2 + tm, :]
        act_scr[:, sl] = _mlp_act(pre, p1, p2, other, cw_ref[:, sl], cb_ref[:, sl],
                                  kind).astype(BF16)
        carry_scr[:, sl] = pre[tm - SUBLANES:tm, :]
    out = x_ref[...] + jnp.dot(act_scr[...], wo_ref[...], preferred_element_type=F32)
    if final_norm:
        out = _rms(out, gf_ref[...])
    xo_ref[...] = out
    st_ref[0] = carry_scr[SUBLANES - 2:SUBLANES, :]


def _mlp_seq(x, g, g_layer, w_parts, conv_w, conv_b, w_layer, w_out_b, pre_small, g_final,
             *, kind, final_norm, tm=512):
    dff = w_out_b.shape[0]
    tiles = SEQ // tm
    row_map = lambda b, t: (b * tiles + t, 0)
    return pl.pallas_call(
        functools.partial(_mlp_seq_kernel, tm=tm, dff=dff, kind=kind, final_norm=final_norm),
        grid=(BATCH, tiles),
        in_specs=[
            pl.BlockSpec((tm, D_MODEL), row_map),
            _const_spec((None, 1, D_MODEL), (g_layer, 0, 0)),
            *[_const_spec((D_MODEL, dff)) for _ in w_parts],
            _const_spec((None, 3, dff), (w_layer, 0, 0)),
            _const_spec((None, 1, dff), (w_layer, 0, 0)),
            _const_spec((dff, D_MODEL)),
            _const_spec((SUBLANES, dff), (N_META // SUBLANES - 1, 0)),
            _const_spec((None, 1, D_MODEL), (0, 0, 0)),
        ],
        out_specs=[
            pl.BlockSpec((tm, D_MODEL), row_map),
            pl.BlockSpec((1, 2, dff), lambda b, t: (b, 0, 0)),
        ],
        out_shape=[
            jax.ShapeDtypeStruct(x.shape, F32),
            jax.ShapeDtypeStruct((BATCH, 2, dff), F32),
        ],
        scratch_shapes=[
            pltpu.VMEM((SUBLANES, dff), F32),
            pltpu.VMEM((tm + SUBLANES, MLP_CW), F32),
            pltpu.VMEM((tm, dff), BF16),
        ],
        compiler_params=_params(2),
        name="mlp_seq_" + kind,
    )(x, g, *w_parts, conv_w, conv_b, w_out_b, pre_small, g_final)


def _mlp_small_kernel(*refs, kind, final_norm):
    npart = _n_parts(kind)
    nb = SMALL_BLOCK
    x_ref, g_ref = refs[0:2]
    w_refs = refs[2:2 + npart]
    (cw_ref, cb_ref, wo_ref, st_ref, gf_ref) = refs[2 + npart:7 + npart]
    xo_ref, pre_ref, sn_ref = refs[7 + npart:10 + npart]
    wb_refs = refs[10 + npart:10 + 2 * npart]
    wob_ref, h_scr, acc_scr, pre_scr = refs[10 + 2 * npart:]
    c = pl.program_id(0)

    @pl.when(c == 0)
    def _():
        h_scr[...] = _rms(x_ref[...], g_ref[...]).astype(BF16)
        acc_scr[...] = jnp.zeros_like(acc_scr)

    parts = []
    for w_ref, wb_ref in zip(w_refs, wb_refs):
        wb = w_ref[...].astype(BF16)
        wb_ref[...] = wb
        parts.append(jnp.dot(h_scr[...], wb, preferred_element_type=F32))
    pre, other = _mlp_pre(parts, kind)
    pre_ref[...] = pre[0:nb]
    s0 = st_ref[:, 0, :]
    s1 = st_ref[:, 1, :]
    sn_ref[:, 0, :] = s1
    sn_ref[:, 1, :] = pre[nb:]
    cw = cw_ref[...]
    cb = cb_ref[...]
    pre_scr[0:SUBLANES, :] = jnp.zeros((SUBLANES, MLP_CW), F32)
    pre_scr[SUBLANES:SUBLANES + nb, :] = pre[0:nb]
    act_meta = _mlp_act(pre[0:nb], pre_scr[SUBLANES - 1:SUBLANES - 1 + nb, :],
                        pre_scr[SUBLANES - 2:SUBLANES - 2 + nb, :], other[0:nb], cw, cb, kind)
    act_samp = _mlp_act(pre[nb:], s1, s0, other[nb:], cw, cb, kind)
    act = jnp.concatenate([act_meta, act_samp], axis=0).astype(BF16)
    wob = wo_ref[...].astype(BF16)
    wob_ref[...] = wob
    acc_scr[...] += jnp.dot(act, wob, preferred_element_type=F32)

    @pl.when(c == pl.num_programs(0) - 1)
    def _():
        out = x_ref[...] + acc_scr[...]
        if final_norm:
            out = _rms(out, gf_ref[...])
        xo_ref[...] = out


def _mlp_small(x, g, g_layer, w_in, conv_w, conv_b, w_out, state, layer, g_final,
               *, kind, final_norm):
    npart = _n_parts(kind)
    dff = w_out.shape[1]
    nc = dff // MLP_CW
    chunk_map = lambda c: (0, c)
    return pl.pallas_call(
        functools.partial(_mlp_small_kernel, kind=kind, final_norm=final_norm),
        grid=(nc,),
        in_specs=[
            _const_spec((SMALL_ROWS, D_MODEL)),
            _const_spec((None, 1, D_MODEL), (g_layer, 0, 0)),
            *[pl.BlockSpec((None, D_MODEL, MLP_CW), functools.partial(
                lambda c, p: (layer, 0, p * nc + c), p=p)) for p in range(npart)],
            pl.BlockSpec((None, 3, MLP_CW), lambda c: (layer, 0, c)),
            pl.BlockSpec((None, 1, MLP_CW), lambda c: (layer, 0, c)),
            pl.BlockSpec((None, MLP_CW, D_MODEL), lambda c: (layer, c, 0)),
            pl.BlockSpec((None, DEC_BATCH, 2, MLP_CW), lambda c: (layer, 0, 0, c)),
            _const_spec((None, 1, D_MODEL), (0, 0, 0)),
        ],
        out_specs=[
            pl.BlockSpec((SMALL_ROWS, D_MODEL), lambda c: (0, 0)),
            pl.BlockSpec((SMALL_BLOCK, MLP_CW), chunk_map),
            pl.BlockSpec((DEC_BATCH, 2, MLP_CW), lambda c: (0, 0, c)),
            *[pl.BlockSpec((D_MODEL, MLP_CW), chunk_map) for _ in range(npart)],
            pl.BlockSpec((MLP_CW, D_MODEL), lambda c: (c, 0)),
        ],
        out_shape=[
            jax.ShapeDtypeStruct((SMALL_ROWS, D_MODEL), F32),
            jax.ShapeDtypeStruct((SMALL_BLOCK, dff), F32),
            jax.ShapeDtypeStruct((DEC_BATCH, 2, dff), F32),
            *[jax.ShapeDtypeStruct((D_MODEL, dff), BF16) for _ in range(npart)],
            jax.ShapeDtypeStruct((dff, D_MODEL), BF16),
        ],
        scratch_shapes=[
            pltpu.VMEM((SMALL_ROWS, D_MODEL), BF16),
            pltpu.VMEM((SMALL_ROWS, D_MODEL), F32),
            pltpu.VMEM((SMALL_BLOCK + SUBLANES, MLP_CW), F32),
        ],
        compiler_params=_params(1),
        name="mlp_small_" + kind,
    )(x, g, *([w_in] * npart), conv_w, conv_b, w_out, state, g_final)


def _rope_tables(pos):
    half = RET_DK // 2
    inv = 1.0 / (ROPE_BASE ** jnp.linspace(0.0, 1.0, half, dtype=F32))
    ang = pos.astype(F32)[:, None] * inv[None, :]
    cos = jnp.repeat(jnp.cos(ang), 2, axis=1)
    sin = jnp.repeat(jnp.sin(ang), 2, axis=1)
    even = (jnp.arange(RET_DK) % 2 == 0)[None, :]
    sa = jnp.where(even, -sin, 0.0)
    sb = jnp.where(even, 0.0, sin)
    return cos, sa, sb


def kernel(x_prompt, x_sample, state_ret, state_conv, state_ffn, meta_tokens, norm_mix, norm_ffn,
           norm_final, w_ret_in, w_ret_out, w_sc_in, w_sc_conv, w_sc_out, w_ffn_in, w_ffn_conv,
           b_ffn_conv, w_ffn_out):
    nb = SMALL_BLOCK
    xb = x_prompt.reshape(BATCH * SEQ, D_MODEL)
    xs = jnp.concatenate([meta_tokens, jnp.zeros((nb - N_META, D_MODEL), F32),
                          x_sample.reshape(DEC_BATCH, D_MODEL)], axis=0)

    g_mix = norm_mix.reshape(2, 1, D_MODEL)
    g_ffn = norm_ffn.reshape(2, 1, D_MODEL)
    g_final = norm_final.reshape(1, 1, D_MODEL)
    b_ffn = b_ffn_conv.reshape(2, 1, D_FF)

    pos_big = N_META + jnp.arange(SEQ, dtype=jnp.int32)
    pos_small = jnp.concatenate([jnp.arange(nb, dtype=jnp.int32),
                                 jnp.full((nb,), PAST_LEN, jnp.int32)])
    tabs_big = _rope_tables(pos_big)
    tabs_small = _rope_tables(pos_small)

    proj_s, w_ret_in_b, qkt_s = _ret_inproj_small(xs, g_mix, w_ret_in, tabs_small)
    xs_meta, s_meta, w_ret_out_b = _ret_meta(proj_s, xs, w_ret_out)
    xb, ret_p, o_s, ret_s = _ret_layer(xb, g_mix, w_ret_in_b, tabs_big, s_meta, w_ret_out_b,
                                       proj_s, qkt_s, state_ret, chunk=256)
    xs_samp = _gn_outproj(o_s, proj_s, xs, w_ret_out_b)
    xs = jnp.concatenate([xs_meta, xs_samp], axis=0)

    def ffn(i, xs, xb, final_norm):
        xs, pre_s, st_s, w_up_b, w_gate_b, w_out_b = _mlp_small(
            xs, g_ffn, i, w_ffn_in, w_ffn_conv, b_ffn, w_ffn_out, state_ffn, i, g_final,
            kind="ffn", final_norm=final_norm)
        xb, st_p = _mlp_seq(xb, g_ffn, i, (w_up_b, w_gate_b), w_ffn_conv, b_ffn, i, w_out_b,
                            pre_s, g_final, kind="ffn", final_norm=final_norm)
        return xs, xb, st_p, st_s

    xs, xb, ffn_p0, ffn_s0 = ffn(0, xs, xb, False)

    zero_b = jnp.zeros((1, 1, D_MODEL), F32)
    xs, pre_s, conv_s, w_bg_b, w_cg_b, w_xin_b, w_sc_out_b = _mlp_small(
        xs, g_mix, 1, w_sc_in, w_sc_conv, zero_b, w_sc_out, state_conv, 0, g_final,
        kind="sc", final_norm=False)
    xb, conv_p = _mlp_seq(xb, g_mix, 1, (w_bg_b, w_cg_b, w_xin_b), w_sc_conv, zero_b, 0,
                          w_sc_out_b, pre_s, g_final, kind="sc", final_norm=False)

    xs, xb, ffn_p1, ffn_s1 = ffn(1, xs, xb, True)

    y_prompt = xb.reshape(BATCH, SEQ, D_MODEL)
    y_sample = xs[nb:].reshape(DEC_BATCH, 1, D_MODEL)
    return (y_prompt, y_sample, ret_p[None], ret_s, conv_p[None], conv_s[None],
            jnp.stack([ffn_p0, ffn_p1]), jnp.stack([ffn_s0, ffn_s1]))
```

```python
import functools
import math

import jax
import jax.numpy as jnp
from jax import lax
from jax.experimental import pallas as pl
from jax.experimental.pallas import tpu as pltpu

F32 = jnp.float32
BF16 = jnp.bfloat16

D_MODEL = 1024
SEQ = 2048
BATCH = 8
DEC_BATCH = 128
PAST_LEN = 16384
N_META = 16
RET_HEADS = 4
RET_DK = 256
RET_DV = 512
RET_VDIM = RET_HEADS * RET_DV
RET_IN = 2 * D_MODEL + 2 * RET_VDIM
ROPE_BASE = 10000.0
D_FF = 2816
NORM_EPS = 1e-6
GN_EPS = 1e-6

LANES = 128
SUBLANES = 8
SMALL_BLOCK = 128
SMALL_ROWS = 2 * SMALL_BLOCK
MLP_CW = 256
PROJ_CW = 512
VMEM_LIMIT = 56 * 1024 * 1024
RET_LAYER_VMEM_LIMIT = 60 * 1024 * 1024


def _params(n_axes):
    return pltpu.CompilerParams(
        dimension_semantics=("arbitrary",) * n_axes,
        vmem_limit_bytes=VMEM_LIMIT)


def _const_spec(shape, index=None):
    index = (0,) * len(shape) if index is None else index
    return pl.BlockSpec(shape, lambda *_: index, pipeline_mode=pl.Buffered(1))


def _rms(x, g):
    ms = jnp.mean(x * x, axis=-1, keepdims=True)
    return x * lax.rsqrt(ms + NORM_EPS) * g


def _rotary(zs, col, c_ref, sa_ref, sb_ref):
    sl = slice(col, col + LANES)
    return (zs * c_ref[:, sl] + pltpu.roll(zs, LANES - 1, 1) * sa_ref[:, sl]
            + pltpu.roll(zs, 1, 1) * sb_ref[:, sl])


def _sample_state_update(p_ref, qkt_ref, s_ref, o_ref, sn_ref, first, count):
    tile = RET_DV // LANES
    for j in range(count):
        b = first + j
        row = p_ref[pl.ds(b, 1), :]
        lane_b = jnp.full((RET_DK, LANES), b, jnp.int32)
        for h in range(RET_HEADS):
            gamma = 1.0 - 2.0 ** (-5.0 - h)
            qs, ks, vs, _ = _head_cols(h)
            v = row[:, vs]
            q_rows = jnp.take_along_axis(qkt_ref[qs, :], lane_b, axis=1)
            k_rows = jnp.take_along_axis(qkt_ref[ks, :], lane_b, axis=1)
            s_prev = s_ref[j, h]
            cross = jnp.sum(s_prev * _lane_tile(q_rows, tile), axis=0, keepdims=True)
            qk = jnp.sum(row[:, qs] * row[:, ks], axis=-1, keepdims=True)
            o_ref[pl.ds(b, 1), h * RET_DV:(h + 1) * RET_DV] = qk * v + gamma * cross
            sn_ref[j, h] = gamma * s_prev + _lane_tile(k_rows, tile) * v


QK_CHUNKS = 2 * D_MODEL // PROJ_CW


def _ret_inproj_small_kernel(x_ref, g_ref, w_ref, c_ref, sa_ref, sb_ref,
                             o_ref, wb_ref, qkt_ref, h_scr):
    c = pl.program_id(0)

    @pl.when(c == 0)
    def _():
        h_scr[...] = _rms(x_ref[...], g_ref[...]).astype(BF16)

    wb = w_ref[...].astype(BF16)
    wb_ref[...] = wb
    z = jnp.dot(h_scr[...], wb, preferred_element_type=F32)

    @pl.when(c >= QK_CHUNKS)
    def _():
        o_ref[...] = z

    @pl.when(c < QK_CHUNKS)
    def _():
        scale = jnp.where(c >= QK_CHUNKS // 2, RET_DK ** -0.5, 1.0)
        for s0 in range(0, PROJ_CW, LANES):
            r = _rotary(z[:, s0:s0 + LANES], s0 % RET_DK, c_ref, sa_ref, sb_ref) * scale
            o_ref[:, s0:s0 + LANES] = r
            qkt_ref[s0:s0 + LANES, :] = r[SMALL_BLOCK:].T


def _ret_inproj_small(x, g, w, tabs):
    return pl.pallas_call(
        _ret_inproj_small_kernel,
        grid=(RET_IN // PROJ_CW,),
        in_specs=[
            _const_spec((SMALL_ROWS, D_MODEL)),
            _const_spec((None, 1, D_MODEL), (0, 0, 0)),
            pl.BlockSpec((None, D_MODEL, PROJ_CW), lambda c: (0, 0, c)),
            _const_spec((SMALL_ROWS, RET_DK)),
            _const_spec((SMALL_ROWS, RET_DK)),
            _const_spec((SMALL_ROWS, RET_DK)),
        ],
        out_specs=[
            pl.BlockSpec((SMALL_ROWS, PROJ_CW), lambda c: (0, c)),
            pl.BlockSpec((D_MODEL, PROJ_CW), lambda c: (0, c)),
            pl.BlockSpec((PROJ_CW, DEC_BATCH), lambda c: (jnp.minimum(c, QK_CHUNKS - 1), 0)),
        ],
        out_shape=[
            jax.ShapeDtypeStruct((SMALL_ROWS, RET_IN), F32),
            jax.ShapeDtypeStruct((D_MODEL, RET_IN), BF16),
            jax.ShapeDtypeStruct((2 * D_MODEL, DEC_BATCH), F32),
        ],
        scratch_shapes=[pltpu.VMEM((SMALL_ROWS, D_MODEL), BF16)],
        compiler_params=_params(1),
        name="ret_inproj_small",
    )(x, g, w, *tabs)


def _gn_gate(o, g):
    mu = jnp.mean(o, axis=-1, keepdims=True)
    d = o - mu
    var = jnp.mean(d * d, axis=-1, keepdims=True)
    on = d * lax.rsqrt(var + GN_EPS)
    gf = g.astype(F32)
    return (gf / (1.0 + jnp.exp(-gf))) * on


def _lane_tile(x, n):
    return jnp.concatenate([x] * n, axis=1)


def _head_cols(h):
    q0 = h * RET_DK
    k0 = D_MODEL + h * RET_DK
    v0 = 2 * D_MODEL + h * RET_DV
    g0 = 2 * D_MODEL + RET_VDIM + h * RET_DV
    return (slice(q0, q0 + RET_DK), slice(k0, k0 + RET_DK),
            slice(v0, v0 + RET_DV), slice(g0, g0 + RET_DV))


def _decay_tables(decay_scr, cdec_scr, kdec_scr, chunk, valid):
    n_mat = lax.broadcasted_iota(jnp.int32, (chunk, chunk), 0)
    m_mat = lax.broadcasted_iota(jnp.int32, (chunk, chunk), 1)
    diff = (n_mat - m_mat).astype(F32)
    causal = diff >= 0
    diff = jnp.where(causal, diff, 0.0)
    n_col = lax.broadcasted_iota(jnp.int32, (chunk, LANES), 0).astype(F32)
    for h in range(RET_HEADS):
        lg = math.log(1.0 - 2.0 ** (-5.0 - h))
        decay_scr[h] = jnp.where(causal, jnp.exp(lg * diff), 0.0)
        cdec_scr[h] = jnp.exp(lg * (n_col + 1.0))
        kdec_scr[h] = jnp.where(n_col < valid, jnp.exp(lg * (valid - 1.0 - n_col)), 0.0)


def _ret_head(h, q, k, v, g, s_ref, decay_scr, cdec_scr, kdec_scr, gated_scr, valid):
    lg = math.log(1.0 - 2.0 ** (-5.0 - h))
    s_prev = s_ref[...]
    scores = lax.dot_general(q, k.astype(BF16), (((1,), (1,)), ((), ())),
                             preferred_element_type=F32)
    inner = jnp.dot((scores * decay_scr[h]).astype(BF16), v, preferred_element_type=F32)
    cross = jnp.dot(q, s_prev.astype(BF16), preferred_element_type=F32)
    o = inner + cross * _lane_tile(cdec_scr[h], RET_DV // LANES)
    kd = (k * _lane_tile(kdec_scr[h], RET_DK // LANES)).astype(BF16)
    kv = lax.dot_general(kd, v, (((0,), (0,)), ((), ())), preferred_element_type=F32)
    s_ref[...] = math.exp(lg * valid) * s_prev + kv
    gated_scr[:, h * RET_DV:(h + 1) * RET_DV] = _gn_gate(o, g).astype(BF16)


def _ret_meta_kernel(p_ref, x_ref, wo_ref, xo_ref, sf_ref, wob_ref,
                     gated_scr, decay_scr, cdec_scr, kdec_scr):
    _decay_tables(decay_scr, cdec_scr, kdec_scr, SMALL_BLOCK, N_META)
    sf_ref[...] = jnp.zeros_like(sf_ref)
    for h in range(RET_HEADS):
        qs, ks, vs, gs = _head_cols(h)
        _ret_head(h, p_ref[:, qs].astype(BF16), p_ref[:, ks], p_ref[:, vs].astype(BF16),
                  p_ref[:, gs], sf_ref.at[h], decay_scr, cdec_scr, kdec_scr, gated_scr, N_META)
    wo = wo_ref[...].astype(BF16)
    wob_ref[...] = wo
    xo_ref[...] = x_ref[...] + jnp.dot(gated_scr[...], wo, preferred_element_type=F32)


def _ret_meta(proj_small, x_small, w_out):
    nb = SMALL_BLOCK
    state_shape = (RET_HEADS, RET_DK, RET_DV)
    w_shape = (RET_VDIM, D_MODEL)
    return pl.pallas_call(
        _ret_meta_kernel,
        grid=(1,),
        in_specs=[
            _const_spec((nb, RET_IN)),
            _const_spec((nb, D_MODEL)),
            _const_spec((None,) + w_shape, (0, 0, 0)),
        ],
        out_specs=[
            pl.BlockSpec((nb, D_MODEL), lambda i: (0, 0)),
            pl.BlockSpec(state_shape, lambda i: (0, 0, 0)),
            pl.BlockSpec(w_shape, lambda i: (0, 0)),
        ],
        out_shape=[
            jax.ShapeDtypeStruct((nb, D_MODEL), F32),
            jax.ShapeDtypeStruct(state_shape, F32),
            jax.ShapeDtypeStruct(w_shape, BF16),
        ],
        scratch_shapes=[
            pltpu.VMEM((nb, RET_VDIM), BF16),
            pltpu.VMEM((RET_HEADS, nb, nb), F32),
            pltpu.VMEM((RET_HEADS, nb, LANES), F32),
            pltpu.VMEM((RET_HEADS, nb, LANES), F32),
        ],
        compiler_params=_params(1),
        name="ret_meta",
    )(proj_small, x_small, w_out)


def _ret_layer_kernel(x_ref, g_ref, w_ref, c_ref, sa_ref, sb_ref, s0_ref, wo_ref, ps_ref,
                      qkt_ref, st_ref, xo_ref, sf_ref, os_ref, sn_ref,
                      gated_scr, decay_scr, cdec_scr, kdec_scr, *, chunk, bb):
    b = pl.program_id(0)
    c = pl.program_id(1)

    @pl.when((b == 0) & (c == 0))
    def _():
        _decay_tables(decay_scr, cdec_scr, kdec_scr, chunk, chunk)

    @pl.when(c == 0)
    def _():
        sf_ref[0] = s0_ref[...]

    step = b * pl.num_programs(1) + c
    _sample_state_update(ps_ref, qkt_ref, st_ref, os_ref, sn_ref, step * bb, bb)

    hn = _rms(x_ref[...], g_ref[...]).astype(BF16)
    proj = lambda sl: jnp.dot(hn, w_ref[:, sl], preferred_element_type=F32)
    rot = lambda z: jnp.concatenate(
        [_rotary(z[:, s0:s0 + LANES], s0, c_ref, sa_ref, sb_ref)
         for s0 in range(0, RET_DK, LANES)], axis=1)
    for h in range(RET_HEADS):
        qs, ks, vs, gs = _head_cols(h)
        q = rot(proj(qs)).astype(BF16)
        k = rot(proj(ks)) * (RET_DK ** -0.5)
        _ret_head(h, q, k, proj(vs).astype(BF16), proj(gs), sf_ref.at[0, h],
                  decay_scr, cdec_scr, kdec_scr, gated_scr, chunk)
    xo_ref[...] = x_ref[...] + jnp.dot(gated_scr[...], wo_ref[...],
                                       preferred_element_type=F32)


def _ret_layer(x, g, w_in_b, tabs, s0, w_out_b, proj_small, qkt_small, state, *, chunk):
    n_chunks = SEQ // chunk
    steps = BATCH * n_chunks
    bb = DEC_BATCH // steps
    row_map = lambda b, c: (b * n_chunks + c, 0)
    state_shape = (RET_HEADS, RET_DK, RET_DV)
    tab_spec = pl.BlockSpec((chunk, RET_DK), lambda b, c: (c, 0))
    samp_spec = pl.BlockSpec((None, bb) + state_shape,
                             lambda b, c: (0, b * n_chunks + c, 0, 0, 0))
    return pl.pallas_call(
        functools.partial(_ret_layer_kernel, chunk=chunk, bb=bb),
        grid=(BATCH, n_chunks),
        in_specs=[
            pl.BlockSpec((chunk, D_MODEL), row_map),
            _const_spec((None, 1, D_MODEL), (0, 0, 0)),
            _const_spec((D_MODEL, RET_IN)),
            tab_spec, tab_spec, tab_spec,
            _const_spec(state_shape),
            _const_spec((RET_VDIM, D_MODEL)),
            _const_spec((SMALL_BLOCK, RET_IN), (1, 0)),
            _const_spec((2 * D_MODEL, DEC_BATCH)),
            samp_spec,
        ],
        out_specs=[
            pl.BlockSpec((chunk, D_MODEL), row_map),
            pl.BlockSpec((1,) + state_shape, lambda b, c: (b, 0, 0, 0)),
            pl.BlockSpec((DEC_BATCH, RET_VDIM), lambda b, c: (0, 0)),
            samp_spec,
        ],
        out_shape=[
            jax.ShapeDtypeStruct(x.shape, F32),
            jax.ShapeDtypeStruct((BATCH,) + state_shape, F32),
            jax.ShapeDtypeStruct((DEC_BATCH, RET_VDIM), F32),
            jax.ShapeDtypeStruct(state.shape, F32),
        ],
        scratch_shapes=[
            pltpu.VMEM((chunk, RET_VDIM), BF16),
            pltpu.VMEM((RET_HEADS, chunk, chunk), F32),
            pltpu.VMEM((RET_HEADS, chunk, LANES), F32),
            pltpu.VMEM((RET_HEADS, chunk, LANES), F32),
        ],
        compiler_params=pltpu.CompilerParams(
            dimension_semantics=("arbitrary", "arbitrary"),
            vmem_limit_bytes=RET_LAYER_VMEM_LIMIT),
        name="ret_layer",
    )(x, g, w_in_b, *tabs, s0, w_out_b, proj_small, qkt_small, state)


def _gn_outproj_kernel(o_ref, g_ref, x_ref, wo_ref, xo_ref):
    parts = []
    for h in range(RET_HEADS):
        sl = slice(h * RET_DV, (h + 1) * RET_DV)
        parts.append(_gn_gate(o_ref[:, sl], g_ref[:, sl]).astype(BF16))
    gated = jnp.concatenate(parts, axis=1)
    xo_ref[...] = x_ref[...] + jnp.dot(gated, wo_ref[...], preferred_element_type=F32)


def _gn_outproj(o, proj_small, x_small, w_out_b):
    return pl.pallas_call(
        _gn_outproj_kernel,
        grid=(1,),
        in_specs=[
            _const_spec((DEC_BATCH, RET_VDIM)),
            _const_spec((SMALL_BLOCK, RET_VDIM), (1, 2)),
            _const_spec((SMALL_BLOCK, D_MODEL), (1, 0)),
            _const_spec((RET_VDIM, D_MODEL)),
        ],
        out_specs=pl.BlockSpec((DEC_BATCH, D_MODEL), lambda i: (0, 0)),
        out_shape=jax.ShapeDtypeStruct((DEC_BATCH, D_MODEL), F32),
        compiler_params=_params(1),
        name="gn_outproj",
    )(o, proj_small, x_small, w_out_b)


def _n_parts(kind):
    return 2 if kind == "ffn" else 3


def _mlp_pre(parts, kind):
    if kind == "ffn":
        return parts[0], parts[1]
    return parts[1] * parts[2], parts[0]


def _mlp_act(pre, p1, p2, other, cw, cb, kind):
    conv = cw[0:1] * p2 + cw[1:2] * p1 + cw[2:3] * pre
    if kind == "ffn":
        a = conv + cb
        return (a / (1.0 + jnp.exp(-a))) * other
    return other * conv


def _mlp_layer_kernel(*refs, tm, dff, kind, final_norm):
    npart = _n_parts(kind)
    nb = SMALL_BLOCK
    nc = dff // MLP_CW
    tiles = SEQ // tm
    xs_ref, xb_ref, g_ref = refs[0:3]
    w_refs = refs[3:3 + npart]
    (cw_ref, cb_ref, wo_ref, st_ref, gf_ref,
     xso_ref, sn_ref, xbo_ref, stp_ref) = refs[3 + npart:12 + npart]
    wb_scrs = refs[12 + npart:12 + 2 * npart]
    (wob_scr, h_scr, acc_scr, meta_carry_scr, carry_scr,
     pre_scr, act_scr) = refs[12 + 2 * npart:]
    i = pl.program_id(0)

    @pl.when(i < nc)
    def _small_step():
        col = pl.multiple_of(i * MLP_CW, MLP_CW)
        cols = pl.ds(col, MLP_CW)

        @pl.when(i == 0)
        def _():
            h_scr[...] = _rms(xs_ref[...], g_ref[...]).astype(BF16)
            acc_scr[...] = jnp.zeros_like(acc_scr)

        parts = []
        for w_ref, wb_scr in zip(w_refs, wb_scrs):
            wb = w_ref[...].astype(BF16)
            wb_scr[:, cols] = wb
            parts.append(jnp.dot(h_scr[...], wb, preferred_element_type=F32))
        pre, other = _mlp_pre(parts, kind)
        cw = cw_ref[:, cols]
        cb = cb_ref[:, cols]
        pre_scr[0:SUBLANES, :] = jnp.zeros((SUBLANES, MLP_CW), F32)
        pre_scr[SUBLANES:SUBLANES + nb, :] = pre[0:nb]
        act_meta = _mlp_act(pre[0:nb], pre_scr[SUBLANES - 1:SUBLANES - 1 + nb, :],
                            pre_scr[SUBLANES - 2:SUBLANES - 2 + nb, :], other[0:nb],
                            cw, cb, kind)
        meta_carry_scr[:, cols] = pre[N_META - SUBLANES:N_META]
        s0 = st_ref[:, 0, :]
        s1 = st_ref[:, 1, :]
        sn_ref[:, 0, :] = s1
        sn_ref[:, 1, :] = pre[nb:]
        act_samp = _mlp_act(pre[nb:], s1, s0, other[nb:], cw, cb, kind)
        act = jnp.concatenate([act_meta, act_samp], axis=0).astype(BF16)
        wob = wo_ref[...].astype(BF16)
        wob_scr[cols, :] = wob
        acc_scr[...] += jnp.dot(act, wob, preferred_element_type=F32)

        @pl.when(i == nc - 1)
        def _():
            out = xs_ref[...] + acc_scr[...]
            if final_norm:
                out = _rms(out, gf_ref[...])
            xso_ref[...] = out

    @pl.when(i >= nc)
    def _big_step():
        @pl.when((i - nc) % tiles == 0)
        def _():
            carry_scr[...] = meta_carry_scr[...]

        h = _rms(xb_ref[...], g_ref[...]).astype(BF16)
        for c0 in range(0, dff, MLP_CW):
            sl = slice(c0, c0 + MLP_CW)
            parts = [jnp.dot(h, w[:, sl], preferred_element_type=F32) for w in wb_scrs]
            pre, other = _mlp_pre(parts, kind)
            pre_scr[0:SUBLANES, :] = carry_scr[:, sl]
            pre_scr[SUBLANES:SUBLANES + tm, :] = pre
            p1 = pre_scr[SUBLANES - 1:SUBLANES - 1 + tm, :]
            p2 = pre_scr[SUBLANES - 2:SUBLANES - 2 + tm, :]
            act_scr[:, sl] = _mlp_act(pre, p1, p2, other, cw_ref[:, sl], cb_ref[:, sl],
                                      kind).astype(BF16)
            carry_scr[:, sl] = pre[tm - SUBLANES:tm, :]
        out = xb_ref[...] + jnp.dot(act_scr[...], wob_scr[...], preferred_element_type=F32)
        if final_norm:
            out = _rms(out, gf_ref[...])
        xbo_ref[...] = out
        stp_ref[0] = carry_scr[SUBLANES - 2:SUBLANES, :]


def _mlp_layer(xs, xb, g, g_layer, w_in, conv_w, conv_b, w_out, state, layer, g_final,
               *, kind, final_norm, tm=512):
    npart = _n_parts(kind)
    dff = w_out.shape[1]
    nc = dff // MLP_CW
    tiles = SEQ // tm
    chunk = lambda i: jnp.minimum(i, nc - 1)
    tile = lambda i: jnp.maximum(i - nc, 0)
    return pl.pallas_call(
        functools.partial(_mlp_layer_kernel, tm=tm, dff=dff, kind=kind, final_norm=final_norm),
        grid=(nc + BATCH * tiles,),
        in_specs=[
            _const_spec((SMALL_ROWS, D_MODEL)),
            pl.BlockSpec((tm, D_MODEL), lambda i: (tile(i), 0)),
            _const_spec((None, 1, D_MODEL), (g_layer, 0, 0)),
            *[pl.BlockSpec((None, D_MODEL, MLP_CW), functools.partial(
                lambda i, p: (layer, 0, p * nc + chunk(i)), p=p)) for p in range(npart)],
            _const_spec((None, 3, dff), (layer, 0, 0)),
            _const_spec((None, 1, dff), (layer, 0, 0)),
            pl.BlockSpec((None, MLP_CW, D_MODEL), lambda i: (layer, chunk(i), 0)),
            pl.BlockSpec((None, DEC_BATCH, 2, MLP_CW), lambda i: (layer, 0, 0, chunk(i))),
            _const_spec((None, 1, D_MODEL), (0, 0, 0)),
        ],
        out_specs=[
            pl.BlockSpec((SMALL_ROWS, D_MODEL), lambda i: (0, 0)),
            pl.BlockSpec((DEC_BATCH, 2, MLP_CW), lambda i: (0, 0, chunk(i))),
            pl.BlockSpec((tm, D_MODEL), lambda i: (tile(i), 0)),
            pl.BlockSpec((1, 2, dff), lambda i: (tile(i) // tiles, 0, 0)),
        ],
        out_shape=[
            jax.ShapeDtypeStruct((SMALL_ROWS, D_MODEL), F32),
            jax.ShapeDtypeStruct((DEC_BATCH, 2, dff), F32),
            jax.ShapeDtypeStruct(xb.shape, F32),
            jax.ShapeDtypeStruct((BATCH, 2, dff), F32),
        ],
        scratch_shapes=[
            *[pltpu.VMEM((D_MODEL, dff), BF16) for _ in range(npart)],
            pltpu.VMEM((dff, D_MODEL), BF16),
            pltpu.VMEM((SMALL_ROWS, D_MODEL), BF16),
            pltpu.VMEM((SMALL_ROWS, D_MODEL), F32),
            pltpu.VMEM((SUBLANES, dff), F32),
            pltpu.VMEM((SUBLANES, dff), F32),
            pltpu.VMEM((tm + SUBLANES, MLP_CW), F32),
            pltpu.VMEM((tm, dff), BF16),
        ],
        compiler_params=_params(1),
        name="mlp_layer_" + kind,
    )(xs, xb, g, *([w_in] * npart), conv_w, conv_b, w_out, state, g_final)


def _rope_tables(pos):
    half = RET_DK // 2
    inv = 1.0 / (ROPE_BASE ** jnp.linspace(0.0, 1.0, half, dtype=F32))
    ang = pos.astype(F32)[:, None] * inv[None, :]
    cos = jnp.repeat(jnp.cos(ang), 2, axis=1)
    sin = jnp.repeat(jnp.sin(ang), 2, axis=1)
    even = (jnp.arange(RET_DK) % 2 == 0)[None, :]
    sa = jnp.where(even, -sin, 0.0)
    sb = jnp.where(even, 0.0, sin)
    return cos, sa, sb


def kernel(x_prompt, x_sample, state_ret, state_conv, state_ffn, meta_tokens, norm_mix, norm_ffn,
           norm_final, w_ret_in, w_ret_out, w_sc_in, w_sc_conv, w_sc_out, w_ffn_in, w_ffn_conv,
           b_ffn_conv, w_ffn_out):
    nb = SMALL_BLOCK
    xb = x_prompt.reshape(BATCH * SEQ, D_MODEL)
    xs = jnp.concatenate([meta_tokens, jnp.zeros((nb - N_META, D_MODEL), F32),
                          x_sample.reshape(DEC_BATCH, D_MODEL)], axis=0)

    g_mix = norm_mix.reshape(2, 1, D_MODEL)
    g_ffn = norm_ffn.reshape(2, 1, D_MODEL)
    g_final = norm_final.reshape(1, 1, D_MODEL)
    b_ffn = b_ffn_conv.reshape(2, 1, D_FF)

    pos_big = N_META + jnp.arange(SEQ, dtype=jnp.int32)
    pos_small = jnp.concatenate([jnp.arange(nb, dtype=jnp.int32),
                                 jnp.full((nb,), PAST_LEN, jnp.int32)])
    tabs_big = _rope_tables(pos_big)
    tabs_small = _rope_tables(pos_small)

    proj_s, w_ret_in_b, qkt_s = _ret_inproj_small(xs, g_mix, w_ret_in, tabs_small)
    xs_meta, s_meta, w_ret_out_b = _ret_meta(proj_s, xs, w_ret_out)
    xb, ret_p, o_s, ret_s = _ret_layer(xb, g_mix, w_ret_in_b, tabs_big, s_meta, w_ret_out_b,
                                       proj_s, qkt_s, state_ret, chunk=256)
    xs_samp = _gn_outproj(o_s, proj_s, xs, w_ret_out_b)
    xs = jnp.concatenate([xs_meta, xs_samp], axis=0)

    def ffn(i, xs, xb, final_norm):
        xs, st_s, xb, st_p = _mlp_layer(
            xs, xb, g_ffn, i, w_ffn_in, w_ffn_conv, b_ffn, w_ffn_out, state_ffn, i, g_final,
            kind="ffn", final_norm=final_norm)
        return xs, xb, st_p, st_s

    xs, xb, ffn_p0, ffn_s0 = ffn(0, xs, xb, False)

    zero_b = jnp.zeros((1, 1, D_MODEL), F32)
    xs, conv_s, xb, conv_p = _mlp_layer(
        xs, xb, g_mix, 1, w_sc_in, w_sc_conv, zero_b, w_sc_out, state_conv, 0, g_final,
        kind="sc", final_norm=False)

    xs, xb, ffn_p1, ffn_s1 = ffn(1, xs, xb, True)

    y_prompt = xb.reshape(BATCH, SEQ, D_MODEL)
    y_sample = xs[nb:].reshape(DEC_BATCH, 1, D_MODEL)
    return (y_prompt, y_sample, ret_p[None], ret_s, conv_p[None], conv_s[None],
            jnp.stack([ffn_p0, ffn_p1]), jnp.stack([ffn_s0, ffn_s1]))
```

```python
import functools
import math

import jax
import jax.numpy as jnp
from jax import lax
from jax.experimental import pallas as pl
from jax.experimental.pallas import tpu as pltpu

F32 = jnp.float32
BF16 = jnp.bfloat16

D_MODEL = 1024
SEQ = 2048
BATCH = 8
DEC_BATCH = 128
PAST_LEN = 16384
N_META = 16
RET_HEADS = 4
RET_DK = 256
RET_DV = 512
RET_VDIM = RET_HEADS * RET_DV
RET_IN = 2 * D_MODEL + 2 * RET_VDIM
ROPE_BASE = 10000.0
D_FF = 2816
NORM_EPS = 1e-6
GN_EPS = 1e-6

LANES = 128
SUBLANES = 8
SMALL_BLOCK = 128
SMALL_ROWS = 2 * SMALL_BLOCK
MLP_CW = 256
PROJ_CW = 512
VMEM_LIMIT = 56 * 1024 * 1024
RET_LAYER_VMEM_LIMIT = 60 * 1024 * 1024


def _params(n_axes):
    return pltpu.CompilerParams(
        dimension_semantics=("arbitrary",) * n_axes,
        vmem_limit_bytes=VMEM_LIMIT)


def _const_spec(shape, index=None):
    index = (0,) * len(shape) if index is None else index
    return pl.BlockSpec(shape, lambda *_: index, pipeline_mode=pl.Buffered(1))


def _rms(x, g):
    ms = jnp.mean(x * x, axis=-1, keepdims=True)
    return x * lax.rsqrt(ms + NORM_EPS) * g


def _rotary(zs, col, c_ref, s_ref):
    sl = slice(col, col + LANES)
    even = lax.broadcasted_iota(jnp.int32, zs.shape, 1) % 2 == 0
    partner = jnp.where(even, pltpu.roll(zs, LANES - 1, 1), pltpu.roll(zs, 1, 1))
    return zs * c_ref[:, sl] + partner * s_ref[:, sl]


def _sample_state_update(p_ref, qkt_ref, s_ref, o_ref, sn_ref, first, count):
    tile = RET_DV // LANES
    for j in range(count):
        b = first + j
        row = p_ref[pl.ds(b, 1), :]
        lane_b = jnp.full((RET_DK, LANES), b, jnp.int32)
        for h in range(RET_HEADS):
            gamma = 1.0 - 2.0 ** (-5.0 - h)
            qs, ks, vs, _ = _head_cols(h)
            v = row[:, vs]
            q_rows = jnp.take_along_axis(qkt_ref[qs, :], lane_b, axis=1)
            k_rows = jnp.take_along_axis(qkt_ref[ks, :], lane_b, axis=1)
            s_prev = s_ref[j, h]
            cross = jnp.sum(s_prev * _lane_tile(q_rows, tile), axis=0, keepdims=True)
            qk = jnp.sum(row[:, qs] * row[:, ks], axis=-1, keepdims=True)
            o_ref[pl.ds(b, 1), h * RET_DV:(h + 1) * RET_DV] = qk * v + gamma * cross
            sn_ref[j, h] = gamma * s_prev + _lane_tile(k_rows, tile) * v


QK_CHUNKS = 2 * D_MODEL // PROJ_CW


def _ret_inproj_small_kernel(x_ref, g_ref, w_ref, c_ref, s_ref,
                             o_ref, wb_ref, qkt_ref, h_scr):
    c = pl.program_id(0)

    @pl.when(c == 0)
    def _():
        h_scr[...] = _rms(x_ref[...], g_ref[...]).astype(BF16)

    wb = w_ref[...].astype(BF16)
    wb_ref[...] = wb
    z = jnp.dot(h_scr[...], wb, preferred_element_type=F32)

    @pl.when(c >= QK_CHUNKS)
    def _():
        o_ref[...] = z

    @pl.when(c < QK_CHUNKS)
    def _():
        scale = jnp.where(c >= QK_CHUNKS // 2, RET_DK ** -0.5, 1.0)
        for s0 in range(0, PROJ_CW, LANES):
            r = _rotary(z[:, s0:s0 + LANES], s0 % RET_DK, c_ref, s_ref) * scale
            o_ref[:, s0:s0 + LANES] = r
            qkt_ref[s0:s0 + LANES, :] = r[SMALL_BLOCK:].T


def _ret_inproj_small(x, g, w, tabs):
    return pl.pallas_call(
        _ret_inproj_small_kernel,
        grid=(RET_IN // PROJ_CW,),
        in_specs=[
            _const_spec((SMALL_ROWS, D_MODEL)),
            _const_spec((None, 1, D_MODEL), (0, 0, 0)),
            pl.BlockSpec((None, D_MODEL, PROJ_CW), lambda c: (0, 0, c)),
            _const_spec((SMALL_ROWS, RET_DK)),
            _const_spec((SMALL_ROWS, RET_DK)),
        ],
        out_specs=[
            pl.BlockSpec((SMALL_ROWS, PROJ_CW), lambda c: (0, c)),
            pl.BlockSpec((D_MODEL, PROJ_CW), lambda c: (0, c)),
            pl.BlockSpec((PROJ_CW, DEC_BATCH), lambda c: (jnp.minimum(c, QK_CHUNKS - 1), 0)),
        ],
        out_shape=[
            jax.ShapeDtypeStruct((SMALL_ROWS, RET_IN), F32),
            jax.ShapeDtypeStruct((D_MODEL, RET_IN), BF16),
            jax.ShapeDtypeStruct((2 * D_MODEL, DEC_BATCH), F32),
        ],
        scratch_shapes=[pltpu.VMEM((SMALL_ROWS, D_MODEL), BF16)],
        compiler_params=_params(1),
        name="ret_inproj_small",
    )(x, g, w, *tabs)


def _gn_gate(o, g):
    mu = jnp.mean(o, axis=-1, keepdims=True)
    d = o - mu
    var = jnp.mean(d * d, axis=-1, keepdims=True)
    on = d * lax.rsqrt(var + GN_EPS)
    gf = g.astype(F32)
    return (gf / (1.0 + jnp.exp(-gf))) * on


def _lane_tile(x, n):
    return jnp.concatenate([x] * n, axis=1)


def _head_cols(h):
    q0 = h * RET_DK
    k0 = D_MODEL + h * RET_DK
    v0 = 2 * D_MODEL + h * RET_DV
    g0 = 2 * D_MODEL + RET_VDIM + h * RET_DV
    return (slice(q0, q0 + RET_DK), slice(k0, k0 + RET_DK),
            slice(v0, v0 + RET_DV), slice(g0, g0 + RET_DV))


def _decay_tables(decay_scr, cdec_scr, kdec_scr, chunk, valid):
    n_mat = lax.broadcasted_iota(jnp.int32, (chunk, chunk), 0)
    m_mat = lax.broadcasted_iota(jnp.int32, (chunk, chunk), 1)
    diff = (n_mat - m_mat).astype(F32)
    causal = diff >= 0
    diff = jnp.where(causal, diff, 0.0)
    n_col = lax.broadcasted_iota(jnp.int32, (chunk, LANES), 0).astype(F32)
    for h in range(RET_HEADS):
        lg = math.log(1.0 - 2.0 ** (-5.0 - h))
        decay_scr[h] = jnp.where(causal, jnp.exp(lg * diff), 0.0)
        cdec_scr[h] = jnp.exp(lg * (n_col + 1.0))
        kdec_scr[h] = jnp.where(n_col < valid, jnp.exp(lg * (valid - 1.0 - n_col)), 0.0)


def _ret_head(h, q, k, v, g, s_ref, decay_scr, cdec_scr, kdec_scr, gated_scr, valid):
    lg = math.log(1.0 - 2.0 ** (-5.0 - h))
    s_prev = s_ref[...]
    scores = lax.dot_general(q, k.astype(BF16), (((1,), (1,)), ((), ())),
                             preferred_element_type=F32)
    inner = jnp.dot((scores * decay_scr[h]).astype(BF16), v, preferred_element_type=F32)
    cross = jnp.dot(q, s_prev.astype(BF16), preferred_element_type=F32)
    o = inner + cross * _lane_tile(cdec_scr[h], RET_DV // LANES)
    kd = (k * _lane_tile(kdec_scr[h], RET_DK // LANES)).astype(BF16)
    kv = lax.dot_general(kd, v, (((0,), (0,)), ((), ())), preferred_element_type=F32)
    s_ref[...] = math.exp(lg * valid) * s_prev + kv
    gated_scr[:, h * RET_DV:(h + 1) * RET_DV] = _gn_gate(o, g).astype(BF16)


def _ret_meta_kernel(p_ref, x_ref, wo_ref, xo_ref, sf_ref, wob_ref,
                     gated_scr, decay_scr, cdec_scr, kdec_scr):
    _decay_tables(decay_scr, cdec_scr, kdec_scr, SMALL_BLOCK, N_META)
    sf_ref[...] = jnp.zeros_like(sf_ref)
    for h in range(RET_HEADS):
        qs, ks, vs, gs = _head_cols(h)
        _ret_head(h, p_ref[:, qs].astype(BF16), p_ref[:, ks], p_ref[:, vs].astype(BF16),
                  p_ref[:, gs], sf_ref.at[h], decay_scr, cdec_scr, kdec_scr, gated_scr, N_META)
    wo = wo_ref[...].astype(BF16)
    wob_ref[...] = wo
    xo_ref[...] = x_ref[...] + jnp.dot(gated_scr[...], wo, preferred_element_type=F32)


def _ret_meta(proj_small, x_small, w_out):
    nb = SMALL_BLOCK
    state_shape = (RET_HEADS, RET_DK, RET_DV)
    w_shape = (RET_VDIM, D_MODEL)
    return pl.pallas_call(
        _ret_meta_kernel,
        grid=(1,),
        in_specs=[
            _const_spec((nb, RET_IN)),
            _const_spec((nb, D_MODEL)),
            _const_spec((None,) + w_shape, (0, 0, 0)),
        ],
        out_specs=[
            pl.BlockSpec((nb, D_MODEL), lambda i: (0, 0)),
            pl.BlockSpec(state_shape, lambda i: (0, 0, 0)),
            pl.BlockSpec(w_shape, lambda i: (0, 0)),
        ],
        out_shape=[
            jax.ShapeDtypeStruct((nb, D_MODEL), F32),
            jax.ShapeDtypeStruct(state_shape, F32),
            jax.ShapeDtypeStruct(w_shape, BF16),
        ],
        scratch_shapes=[
            pltpu.VMEM((nb, RET_VDIM), BF16),
            pltpu.VMEM((RET_HEADS, nb, nb), F32),
            pltpu.VMEM((RET_HEADS, nb, LANES), F32),
            pltpu.VMEM((RET_HEADS, nb, LANES), F32),
        ],
        compiler_params=_params(1),
        name="ret_meta",
    )(proj_small, x_small, w_out)


def _ret_layer_kernel(x_ref, g_ref, w_ref, c_ref, s_ref, s0_ref, wo_ref, ps_ref,
                      qkt_ref, st_ref, xo_ref, sf_ref, os_ref, sn_ref,
                      gated_scr, decay_scr, cdec_scr, kdec_scr, *, chunk, bb):
    b = pl.program_id(0)
    c = pl.program_id(1)

    @pl.when((b == 0) & (c == 0))
    def _():
        _decay_tables(decay_scr, cdec_scr, kdec_scr, chunk, chunk)

    @pl.when(c == 0)
    def _():
        sf_ref[0] = s0_ref[...]

    step = b * pl.num_programs(1) + c
    _sample_state_update(ps_ref, qkt_ref, st_ref, os_ref, sn_ref, step * bb, bb)

    hn = _rms(x_ref[...], g_ref[...]).astype(BF16)
    proj = lambda sl: jnp.dot(hn, w_ref[:, sl], preferred_element_type=F32)
    rot = lambda z: jnp.concatenate(
        [_rotary(z[:, s0:s0 + LANES], s0, c_ref, s_ref)
         for s0 in range(0, RET_DK, LANES)], axis=1)
    for h in range(RET_HEADS):
        qs, ks, vs, gs = _head_cols(h)
        q = rot(proj(qs)).astype(BF16)
        k = rot(proj(ks)) * (RET_DK ** -0.5)
        _ret_head(h, q, k, proj(vs).astype(BF16), proj(gs), sf_ref.at[0, h],
                  decay_scr, cdec_scr, kdec_scr, gated_scr, chunk)
    xo_ref[...] = x_ref[...] + jnp.dot(gated_scr[...], wo_ref[...],
                                       preferred_element_type=F32)


def _ret_layer(x, g, w_in_b, tabs, s0, w_out_b, proj_small, qkt_small, state, *, chunk):
    n_chunks = SEQ // chunk
    steps = BATCH * n_chunks
    bb = DEC_BATCH // steps
    row_map = lambda b, c: (b * n_chunks + c, 0)
    state_shape = (RET_HEADS, RET_DK, RET_DV)
    tab_spec = pl.BlockSpec((chunk, RET_DK), lambda b, c: (c, 0))
    samp_spec = pl.BlockSpec((None, bb) + state_shape,
                             lambda b, c: (0, b * n_chunks + c, 0, 0, 0))
    return pl.pallas_call(
        functools.partial(_ret_layer_kernel, chunk=chunk, bb=bb),
        grid=(BATCH, n_chunks),
        in_specs=[
            pl.BlockSpec((chunk, D_MODEL), row_map),
            _const_spec((None, 1, D_MODEL), (0, 0, 0)),
            _const_spec((D_MODEL, RET_IN)),
            tab_spec, tab_spec,
            _const_spec(state_shape),
            _const_spec((RET_VDIM, D_MODEL)),
            _const_spec((SMALL_BLOCK, RET_IN), (1, 0)),
            _const_spec((2 * D_MODEL, DEC_BATCH)),
            samp_spec,
        ],
        out_specs=[
            pl.BlockSpec((chunk, D_MODEL), row_map),
            pl.BlockSpec((1,) + state_shape, lambda b, c: (b, 0, 0, 0)),
            pl.BlockSpec((DEC_BATCH, RET_VDIM), lambda b, c: (0, 0)),
            samp_spec,
        ],
        out_shape=[
            jax.ShapeDtypeStruct(x.shape, F32),
            jax.ShapeDtypeStruct((BATCH,) + state_shape, F32),
            jax.ShapeDtypeStruct((DEC_BATCH, RET_VDIM), F32),
            jax.ShapeDtypeStruct(state.shape, F32),
        ],
        scratch_shapes=[
            pltpu.VMEM((chunk, RET_VDIM), BF16),
            pltpu.VMEM((RET_HEADS, chunk, chunk), F32),
            pltpu.VMEM((RET_HEADS, chunk, LANES), F32),
            pltpu.VMEM((RET_HEADS, chunk, LANES), F32),
        ],
        compiler_params=pltpu.CompilerParams(
            dimension_semantics=("arbitrary", "arbitrary"),
            vmem_limit_bytes=RET_LAYER_VMEM_LIMIT),
        name="ret_layer",
    )(x, g, w_in_b, *tabs, s0, w_out_b, proj_small, qkt_small, state)


def _gn_outproj_kernel(o_ref, g_ref, x_ref, wo_ref, xo_ref):
    parts = []
    for h in range(RET_HEADS):
        sl = slice(h * RET_DV, (h + 1) * RET_DV)
        parts.append(_gn_gate(o_ref[:, sl], g_ref[:, sl]).astype(BF16))
    gated = jnp.concatenate(parts, axis=1)
    xo_ref[...] = x_ref[...] + jnp.dot(gated, wo_ref[...], preferred_element_type=F32)


def _gn_outproj(o, proj_small, x_small, w_out_b):
    return pl.pallas_call(
        _gn_outproj_kernel,
        grid=(1,),
        in_specs=[
            _const_spec((DEC_BATCH, RET_VDIM)),
            _const_spec((SMALL_BLOCK, RET_VDIM), (1, 2)),
            _const_spec((SMALL_BLOCK, D_MODEL), (1, 0)),
            _const_spec((RET_VDIM, D_MODEL)),
        ],
        out_specs=pl.BlockSpec((DEC_BATCH, D_MODEL), lambda i: (0, 0)),
        out_shape=jax.ShapeDtypeStruct((DEC_BATCH, D_MODEL), F32),
        compiler_params=_params(1),
        name="gn_outproj",
    )(o, proj_small, x_small, w_out_b)


def _n_parts(kind):
    return 2 if kind == "ffn" else 3


def _mlp_pre(parts, kind):
    if kind == "ffn":
        return parts[0], parts[1]
    return parts[1] * parts[2], parts[0]


def _mlp_act(pre, p1, p2, other, cw, cb, kind):
    conv = cw[0:1] * p2 + cw[1:2] * p1 + cw[2:3] * pre
    if kind == "ffn":
        a = conv + cb
        return (a / (1.0 + jnp.exp(-a))) * other
    return other * conv


def _mlp_layer_kernel(*refs, tm, dff, kind, final_norm):
    npart = _n_parts(kind)
    nb = SMALL_BLOCK
    nc = dff // MLP_CW
    tiles = SEQ // tm
    xs_ref, xb_ref, g_ref = refs[0:3]
    w_refs = refs[3:3 + npart]
    (cw_ref, cb_ref, wo_ref, st_ref, gf_ref,
     xso_ref, sn_ref, xbo_ref, stp_ref) = refs[3 + npart:12 + npart]
    wb_scrs = refs[12 + npart:12 + 2 * npart]
    (wob_scr, h_scr, acc_scr, meta_carry_scr, carry_scr,
     pre_scr, act_scr) = refs[12 + 2 * npart:]
    i = pl.program_id(0)

    @pl.when(i < nc)
    def _small_step():
        col = pl.multiple_of(i * MLP_CW, MLP_CW)
        cols = pl.ds(col, MLP_CW)

        @pl.when(i == 0)
        def _():
            h_scr[...] = _rms(xs_ref[...], g_ref[...]).astype(BF16)
            acc_scr[...] = jnp.zeros_like(acc_scr)

        parts = []
        for w_ref, wb_scr in zip(w_refs, wb_scrs):
            wb = w_ref[...].astype(BF16)
            wb_scr[:, cols] = wb
            parts.append(jnp.dot(h_scr[...], wb, preferred_element_type=F32))
        pre, other = _mlp_pre(parts, kind)
        cw = cw_ref[:, cols]
        cb = cb_ref[:, cols]
        pre_scr[0:SUBLANES, :] = jnp.zeros((SUBLANES, MLP_CW), F32)
        pre_scr[SUBLANES:SUBLANES + nb, :] = pre[0:nb]
        act_meta = _mlp_act(pre[0:nb], pre_scr[SUBLANES - 1:SUBLANES - 1 + nb, :],
                            pre_scr[SUBLANES - 2:SUBLANES - 2 + nb, :], other[0:nb],
                            cw, cb, kind)
        meta_carry_scr[:, cols] = pre[N_META - SUBLANES:N_META]
        s0 = st_ref[:, 0, :]
        s1 = st_ref[:, 1, :]
        sn_ref[:, 0, :] = s1
        sn_ref[:, 1, :] = pre[nb:]
        act_samp = _mlp_act(pre[nb:], s1, s0, other[nb:], cw, cb, kind)
        act = jnp.concatenate([act_meta, act_samp], axis=0).astype(BF16)
        wob = wo_ref[...].astype(BF16)
        wob_scr[cols, :] = wob
        acc_scr[...] += jnp.dot(act, wob, preferred_element_type=F32)

        @pl.when(i == nc - 1)
        def _():
            out = xs_ref[...] + acc_scr[...]
            if final_norm:
                out = _rms(out, gf_ref[...])
            xso_ref[...] = out

    @pl.when(i >= nc)
    def _big_step():
        @pl.when((i - nc) % tiles == 0)
        def _():
            carry_scr[...] = meta_carry_scr[...]

        h = _rms(xb_ref[...], g_ref[...]).astype(BF16)
        for c0 in range(0, dff, MLP_CW):
            sl = slice(c0, c0 + MLP_CW)
            parts = [jnp.dot(h, w[:, sl], preferred_element_type=F32) for w in wb_scrs]
            pre, other = _mlp_pre(parts, kind)
            pre_scr[0:SUBLANES, :] = carry_scr[:, sl]
            pre_scr[SUBLANES:SUBLANES + tm, :] = pre
            p1 = pre_scr[SUBLANES - 1:SUBLANES - 1 + tm, :]
            p2 = pre_scr[SUBLANES - 2:SUBLANES - 2 + tm, :]
            act_scr[:, sl] = _mlp_act(pre, p1, p2, other, cw_ref[:, sl], cb_ref[:, sl],
                                      kind).astype(BF16)
            carry_scr[:, sl] = pre[tm - SUBLANES:tm, :]
        out = xb_ref[...] + jnp.dot(act_scr[...], wob_scr[...], preferred_element_type=F32)
        if final_norm:
            out = _rms(out, gf_ref[...])
        xbo_ref[...] = out
        stp_ref[0] = carry_scr[SUBLANES - 2:SUBLANES, :]


def _mlp_layer(xs, xb, g, g_layer, w_in, conv_w, conv_b, w_out, state, layer, g_final,
               *, kind, final_norm, tm=1024):
    npart = _n_parts(kind)
    dff = w_out.shape[1]
    nc = dff // MLP_CW
    tiles = SEQ // tm
    chunk = lambda i: jnp.minimum(i, nc - 1)
    tile = lambda i: jnp.maximum(i - nc, 0)
    return pl.pallas_call(
        functools.partial(_mlp_layer_kernel, tm=tm, dff=dff, kind=kind, final_norm=final_norm),
        grid=(nc + BATCH * tiles,),
        in_specs=[
            _const_spec((SMALL_ROWS, D_MODEL)),
            pl.BlockSpec((tm, D_MODEL), lambda i: (tile(i), 0)),
            _const_spec((None, 1, D_MODEL), (g_layer, 0, 0)),
            *[pl.BlockSpec((None, D_MODEL, MLP_CW), functools.partial(
                lambda i, p: (layer, 0, p * nc + chunk(i)), p=p)) for p in range(npart)],
            _const_spec((None, 3, dff), (layer, 0, 0)),
            _const_spec((None, 1, dff), (layer, 0, 0)),
            pl.BlockSpec((None, MLP_CW, D_MODEL), lambda i: (layer, chunk(i), 0)),
            pl.BlockSpec((None, DEC_BATCH, 2, MLP_CW), lambda i: (layer, 0, 0, chunk(i))),
            _const_spec((None, 1, D_MODEL), (0, 0, 0)),
        ],
        out_specs=[
            pl.BlockSpec((SMALL_ROWS, D_MODEL), lambda i: (0, 0)),
            pl.BlockSpec((DEC_BATCH, 2, MLP_CW), lambda i: (0, 0, chunk(i))),
            pl.BlockSpec((tm, D_MODEL), lambda i: (tile(i), 0)),
            pl.BlockSpec((1, 2, dff), lambda i: (tile(i) // tiles, 0, 0)),
        ],
        out_shape=[
            jax.ShapeDtypeStruct((SMALL_ROWS, D_MODEL), F32),
            jax.ShapeDtypeStruct((DEC_BATCH, 2, dff), F32),
            jax.ShapeDtypeStruct(xb.shape, F32),
            jax.ShapeDtypeStruct((BATCH, 2, dff), F32),
        ],
        scratch_shapes=[
            *[pltpu.VMEM((D_MODEL, dff), BF16) for _ in range(npart)],
            pltpu.VMEM((dff, D_MODEL), BF16),
            pltpu.VMEM((SMALL_ROWS, D_MODEL), BF16),
            pltpu.VMEM((SMALL_ROWS, D_MODEL), F32),
            pltpu.VMEM((SUBLANES, dff), F32),
            pltpu.VMEM((SUBLANES, dff), F32),
            pltpu.VMEM((tm + SUBLANES, MLP_CW), F32),
            pltpu.VMEM((tm, dff), BF16),
        ],
        compiler_params=_params(1),
        name="mlp_layer_" + kind,
    )(xs, xb, g, *([w_in] * npart), conv_w, conv_b, w_out, state, g_final)


def _rope_tables(pos):
    half = RET_DK // 2
    inv = 1.0 / (ROPE_BASE ** jnp.linspace(0.0, 1.0, half, dtype=F32))
    ang = pos.astype(F32)[:, None] * jnp.repeat(inv, 2)[None, :]
    sign = jnp.where(jnp.arange(RET_DK) % 2 == 0, -1.0, 1.0).astype(F32)
    return jnp.cos(ang), jnp.sin(ang) * sign[None, :]


def kernel(x_prompt, x_sample, state_ret, state_conv, state_ffn, meta_tokens, norm_mix, norm_ffn,
           norm_final, w_ret_in, w_ret_out, w_sc_in, w_sc_conv, w_sc_out, w_ffn_in, w_ffn_conv,
           b_ffn_conv, w_ffn_out):
    nb = SMALL_BLOCK
    xb = x_prompt.reshape(BATCH * SEQ, D_MODEL)
    xs = jnp.concatenate([meta_tokens, jnp.zeros((nb - N_META, D_MODEL), F32),
                          x_sample.reshape(DEC_BATCH, D_MODEL)], axis=0)

    g_mix = norm_mix.reshape(2, 1, D_MODEL)
    g_ffn = norm_ffn.reshape(2, 1, D_MODEL)
    g_final = norm_final.reshape(1, 1, D_MODEL)
    b_ffn = b_ffn_conv.reshape(2, 1, D_FF)

    pos_big = N_META + jnp.arange(SEQ, dtype=jnp.int32)
    pos_small = jnp.concatenate([jnp.arange(nb, dtype=jnp.int32),
                                 jnp.full((nb,), PAST_LEN, jnp.int32)])
    tabs_big = _rope_tables(pos_big)
    tabs_small = _rope_tables(pos_small)

    proj_s, w_ret_in_b, qkt_s = _ret_inproj_small(xs, g_mix, w_ret_in, tabs_small)
    xs_meta, s_meta, w_ret_out_b = _ret_meta(proj_s, xs, w_ret_out)
    xb, ret_p, o_s, ret_s = _ret_layer(xb, g_mix, w_ret_in_b, tabs_big, s_meta, w_ret_out_b,
                                       proj_s, qkt_s, state_ret, chunk=256)
    xs_samp = _gn_outproj(o_s, proj_s, xs, w_ret_out_b)
    xs = jnp.concatenate([xs_meta, xs_samp], axis=0)

    def ffn(i, xs, xb, final_norm):
        xs, st_s, xb, st_p = _mlp_layer(
            xs, xb, g_ffn, i, w_ffn_in, w_ffn_conv, b_ffn, w_ffn_out, state_ffn, i, g_final,
            kind="ffn", final_norm=final_norm)
        return xs, xb, st_p, st_s

    xs, xb, ffn_p0, ffn_s0 = ffn(0, xs, xb, False)

    zero_b = jnp.zeros((1, 1, D_MODEL), F32)
    xs, conv_s, xb, conv_p = _mlp_layer(
        xs, xb, g_mix, 1, w_sc_in, w_sc_conv, zero_b, w_sc_out, state_conv, 0, g_final,
        kind="sc", final_norm=False)

    xs, xb, ffn_p1, ffn_s1 = ffn(1, xs, xb, True)

    y_prompt = xb.reshape(BATCH, SEQ, D_MODEL)
    y_sample = xs[nb:].reshape(DEC_BATCH, 1, D_MODEL)
    return (y_prompt, y_sample, ret_p[None], ret_s, conv_p[None], conv_s[None],
            jnp.stack([ffn_p0, ffn_p1]), jnp.stack([ffn_s0, ffn_s1]))
```

```python
import functools
import math

import jax
import jax.numpy as jnp
from jax import lax
from jax.experimental import pallas as pl
from jax.experimental.pallas import tpu as pltpu

F32 = jnp.float32
BF16 = jnp.bfloat16

D_MODEL = 1024
SEQ = 2048
BATCH = 8
DEC_BATCH = 128
PAST_LEN = 16384
N_META = 16
RET_HEADS = 4
RET_DK = 256
RET_DV = 512
RET_VDIM = RET_HEADS * RET_DV
RET_IN = 2 * D_MODEL + 2 * RET_VDIM
ROPE_BASE = 10000.0
D_FF = 2816
NORM_EPS = 1e-6
GN_EPS = 1e-6

LANES = 128
SUBLANES = 8
SMALL_BLOCK = 128
SMALL_ROWS = 2 * SMALL_BLOCK
MLP_CW = 256
PROJ_CW = 512
ROT_LO = 64
VMEM_LIMIT = 56 * 1024 * 1024
RET_LAYER_VMEM_LIMIT = 60 * 1024 * 1024


def _params(n_axes):
    return pltpu.CompilerParams(
        dimension_semantics=("arbitrary",) * n_axes,
        vmem_limit_bytes=VMEM_LIMIT)


def _const_spec(shape, index=None):
    index = (0,) * len(shape) if index is None else index
    return pl.BlockSpec(shape, lambda *_: index, pipeline_mode=pl.Buffered(1))


def _rms(x, g):
    ms = jnp.mean(x * x, axis=-1, keepdims=True)
    return x * lax.rsqrt(ms + NORM_EPS) * g


def _rotary(zs, col, c_ref, s_ref):
    sl = slice(col, col + LANES)
    even = lax.broadcasted_iota(jnp.int32, zs.shape, 1) % 2 == 0
    partner = jnp.where(even, pltpu.roll(zs, LANES - 1, 1), pltpu.roll(zs, 1, 1))
    return zs * c_ref[:, sl] + partner * s_ref[:, sl]


def _sample_state_update(p_ref, qkt_ref, s_ref, o_ref, sn_ref, first, count):
    tile = RET_DV // LANES
    for j in range(count):
        b = first + j
        row = p_ref[pl.ds(b, 1), :]
        lane_b = jnp.full((RET_DK, LANES), b, jnp.int32)
        for h in range(RET_HEADS):
            gamma = 1.0 - 2.0 ** (-5.0 - h)
            qs, ks, vs, _ = _head_cols(h)
            v = row[:, vs]
            q_rows = jnp.take_along_axis(qkt_ref[qs, :], lane_b, axis=1)
            k_rows = jnp.take_along_axis(qkt_ref[ks, :], lane_b, axis=1)
            s_prev = s_ref[j, h]
            cross = jnp.sum(s_prev * _lane_tile(q_rows, tile), axis=0, keepdims=True)
            qk = jnp.sum(row[:, qs] * row[:, ks], axis=-1, keepdims=True)
            o_ref[pl.ds(b, 1), h * RET_DV:(h + 1) * RET_DV] = qk * v + gamma * cross
            sn_ref[j, h] = gamma * s_prev + _lane_tile(k_rows, tile) * v


QK_CHUNKS = 2 * D_MODEL // PROJ_CW


def _ret_inproj_small_kernel(xm_ref, xs_ref, g_ref, w_ref, c_ref, s_ref,
                             o_ref, wb_ref, qkt_ref, h_scr):
    c = pl.program_id(0)

    @pl.when(c == 0)
    def _():
        h_scr[0:N_META, :] = _rms(xm_ref[...], g_ref[...]).astype(BF16)
        h_scr[N_META:SMALL_BLOCK, :] = jnp.zeros((SMALL_BLOCK - N_META, D_MODEL), BF16)
        h_scr[SMALL_BLOCK:, :] = _rms(xs_ref[...], g_ref[...]).astype(BF16)

    wb = w_ref[...].astype(BF16)
    wb_ref[...] = wb
    z = jnp.dot(h_scr[...], wb, preferred_element_type=F32)

    @pl.when(c >= QK_CHUNKS)
    def _():
        o_ref[...] = z

    @pl.when(c < QK_CHUNKS)
    def _():
        scale = jnp.where(c >= QK_CHUNKS // 2, RET_DK ** -0.5, 1.0)
        for s0 in range(0, PROJ_CW, LANES):
            r = _rotary(z[:, s0:s0 + LANES], s0 % RET_DK, c_ref, s_ref) * scale
            o_ref[:, s0:s0 + LANES] = r
            qkt_ref[s0:s0 + LANES, :] = r[SMALL_BLOCK:].T


def _ret_inproj_small(x_meta, x_samp, g, w, tabs):
    return pl.pallas_call(
        _ret_inproj_small_kernel,
        grid=(RET_IN // PROJ_CW,),
        in_specs=[
            _const_spec((N_META, D_MODEL)),
            _const_spec((DEC_BATCH, D_MODEL)),
            _const_spec((None, 1, D_MODEL), (0, 0, 0)),
            pl.BlockSpec((None, D_MODEL, PROJ_CW), lambda c: (0, 0, c)),
            _const_spec((SMALL_ROWS, RET_DK)),
            _const_spec((SMALL_ROWS, RET_DK)),
        ],
        out_specs=[
            pl.BlockSpec((SMALL_ROWS, PROJ_CW), lambda c: (0, c)),
            pl.BlockSpec((D_MODEL, PROJ_CW), lambda c: (0, c)),
            pl.BlockSpec((PROJ_CW, DEC_BATCH), lambda c: (jnp.minimum(c, QK_CHUNKS - 1), 0)),
        ],
        out_shape=[
            jax.ShapeDtypeStruct((SMALL_ROWS, RET_IN), F32),
            jax.ShapeDtypeStruct((D_MODEL, RET_IN), BF16),
            jax.ShapeDtypeStruct((2 * D_MODEL, DEC_BATCH), F32),
        ],
        scratch_shapes=[pltpu.VMEM((SMALL_ROWS, D_MODEL), BF16)],
        compiler_params=_params(1),
        name="ret_inproj_small",
    )(x_meta, x_samp, g, w, *tabs)


def _gn_gate(o, g):
    mu = jnp.mean(o, axis=-1, keepdims=True)
    d = o - mu
    var = jnp.mean(d * d, axis=-1, keepdims=True)
    on = d * lax.rsqrt(var + GN_EPS)
    gf = g.astype(F32)
    return (gf / (1.0 + jnp.exp(-gf))) * on


def _lane_tile(x, n):
    return jnp.concatenate([x] * n, axis=1)


def _head_cols(h):
    q0 = h * RET_DK
    k0 = D_MODEL + h * RET_DK
    v0 = 2 * D_MODEL + h * RET_DV
    g0 = 2 * D_MODEL + RET_VDIM + h * RET_DV
    return (slice(q0, q0 + RET_DK), slice(k0, k0 + RET_DK),
            slice(v0, v0 + RET_DV), slice(g0, g0 + RET_DV))


def _decay_tables(decay_scr, cdec_scr, kdec_scr, chunk, valid):
    n_mat = lax.broadcasted_iota(jnp.int32, (chunk, chunk), 0)
    m_mat = lax.broadcasted_iota(jnp.int32, (chunk, chunk), 1)
    diff = (n_mat - m_mat).astype(F32)
    causal = diff >= 0
    diff = jnp.where(causal, diff, 0.0)
    n_col = lax.broadcasted_iota(jnp.int32, (chunk, LANES), 0).astype(F32)
    for h in range(RET_HEADS):
        lg = math.log(1.0 - 2.0 ** (-5.0 - h))
        decay_scr[h] = jnp.where(causal, jnp.exp(lg * diff), 0.0)
        cdec_scr[h] = jnp.exp(lg * (n_col + 1.0))
        kdec_scr[h] = jnp.where(n_col < valid, jnp.exp(lg * (valid - 1.0 - n_col)), 0.0)


def _ret_head(h, q, k, v, g, s_ref, decay_scr, cdec_scr, kdec_scr, gated_scr, valid):
    lg = math.log(1.0 - 2.0 ** (-5.0 - h))
    s_prev = s_ref[...]
    scores = lax.dot_general(q, k.astype(BF16), (((1,), (1,)), ((), ())),
                             preferred_element_type=F32)
    inner = jnp.dot((scores * decay_scr[h]).astype(BF16), v, preferred_element_type=F32)
    cross = jnp.dot(q, s_prev.astype(BF16), preferred_element_type=F32)
    o = inner + cross * _lane_tile(cdec_scr[h], RET_DV // LANES)
    kd = (k * _lane_tile(kdec_scr[h], RET_DK // LANES)).astype(BF16)
    kv = lax.dot_general(kd, v, (((0,), (0,)), ((), ())), preferred_element_type=F32)
    s_ref[...] = math.exp(lg * valid) * s_prev + kv
    gated_scr[:, h * RET_DV:(h + 1) * RET_DV] = _gn_gate(o, g).astype(BF16)


def _ret_meta_kernel(p_ref, x_ref, wo_ref, xo_ref, sf_ref, wob_ref,
                     gated_scr, decay_scr, cdec_scr, kdec_scr):
    _decay_tables(decay_scr, cdec_scr, kdec_scr, SMALL_BLOCK, N_META)
    sf_ref[...] = jnp.zeros_like(sf_ref)
    for h in range(RET_HEADS):
        qs, ks, vs, gs = _head_cols(h)
        _ret_head(h, p_ref[:, qs].astype(BF16), p_ref[:, ks], p_ref[:, vs].astype(BF16),
                  p_ref[:, gs], sf_ref.at[h], decay_scr, cdec_scr, kdec_scr, gated_scr, N_META)
    wo = wo_ref[...].astype(BF16)
    wob_ref[...] = wo
    out = jnp.dot(gated_scr[...], wo, preferred_element_type=F32)
    xo_ref[0:N_META, :] = x_ref[...] + out[0:N_META]
    xo_ref[N_META:, :] = out[N_META:]


def _ret_meta(proj_small, x_meta, w_out):
    nb = SMALL_BLOCK
    state_shape = (RET_HEADS, RET_DK, RET_DV)
    w_shape = (RET_VDIM, D_MODEL)
    return pl.pallas_call(
        _ret_meta_kernel,
        grid=(1,),
        in_specs=[
            _const_spec((nb, RET_IN)),
            _const_spec((N_META, D_MODEL)),
            _const_spec((None,) + w_shape, (0, 0, 0)),
        ],
        out_specs=[
            pl.BlockSpec((nb, D_MODEL), lambda i: (0, 0)),
            pl.BlockSpec(state_shape, lambda i: (0, 0, 0)),
            pl.BlockSpec(w_shape, lambda i: (0, 0)),
        ],
        out_shape=[
            jax.ShapeDtypeStruct((nb, D_MODEL), F32),
            jax.ShapeDtypeStruct(state_shape, F32),
            jax.ShapeDtypeStruct(w_shape, BF16),
        ],
        scratch_shapes=[
            pltpu.VMEM((nb, RET_VDIM), BF16),
            pltpu.VMEM((RET_HEADS, nb, nb), F32),
            pltpu.VMEM((RET_HEADS, nb, LANES), F32),
            pltpu.VMEM((RET_HEADS, nb, LANES), F32),
        ],
        compiler_params=_params(1),
        name="ret_meta",
    )(proj_small, x_meta, w_out)


def _ret_layer_kernel(x_ref, g_ref, w_ref, ta_ref, tb_ref, s0_ref, wo_ref, ps_ref,
                      qkt_ref, st_ref, xo_ref, sf_ref, os_ref, sn_ref,
                      gated_scr, decay_scr, cdec_scr, kdec_scr, c_ref, s_ref, *, chunk, bb):
    b = pl.program_id(0)
    c = pl.program_id(1)

    @pl.when((b == 0) & (c == 0))
    def _():
        _decay_tables(decay_scr, cdec_scr, kdec_scr, chunk, chunk)

    for blk in range(chunk // ROT_LO):
        rows = slice(blk * ROT_LO, (blk + 1) * ROT_LO)
        hi = pl.ds(c * (chunk // ROT_LO) + blk, 1)
        ca, sa = ta_ref[0, hi, :], ta_ref[1, hi, :]
        c_ref[rows, :] = ca * tb_ref[0] - sa * tb_ref[1]
        s_ref[rows, :] = sa * tb_ref[2] + ca * tb_ref[3]

    @pl.when(c == 0)
    def _():
        sf_ref[0] = s0_ref[...]

    step = b * pl.num_programs(1) + c
    _sample_state_update(ps_ref, qkt_ref, st_ref, os_ref, sn_ref, step * bb, bb)

    hn = _rms(x_ref[...], g_ref[...]).astype(BF16)
    proj = lambda sl: jnp.dot(hn, w_ref[:, sl], preferred_element_type=F32)
    rot = lambda z: jnp.concatenate(
        [_rotary(z[:, s0:s0 + LANES], s0, c_ref, s_ref)
         for s0 in range(0, RET_DK, LANES)], axis=1)
    for h in range(RET_HEADS):
        qs, ks, vs, gs = _head_cols(h)
        q = rot(proj(qs)).astype(BF16)
        k = rot(proj(ks)) * (RET_DK ** -0.5)
        _ret_head(h, q, k, proj(vs).astype(BF16), proj(gs), sf_ref.at[0, h],
                  decay_scr, cdec_scr, kdec_scr, gated_scr, chunk)
    xo_ref[...] = x_ref[...] + jnp.dot(gated_scr[...], wo_ref[...],
                                       preferred_element_type=F32)


def _ret_layer(x, g, w_in_b, tabs, s0, w_out_b, proj_small, qkt_small, state, *, chunk):
    n_chunks = SEQ // chunk
    steps = BATCH * n_chunks
    bb = DEC_BATCH // steps
    row_map = lambda b, c: (b * n_chunks + c, 0)
    state_shape = (RET_HEADS, RET_DK, RET_DV)
    samp_spec = pl.BlockSpec((None, bb) + state_shape,
                             lambda b, c: (0, b * n_chunks + c, 0, 0, 0))
    return pl.pallas_call(
        functools.partial(_ret_layer_kernel, chunk=chunk, bb=bb),
        grid=(BATCH, n_chunks),
        in_specs=[
            pl.BlockSpec((chunk, D_MODEL), row_map),
            _const_spec((None, 1, D_MODEL), (0, 0, 0)),
            _const_spec((D_MODEL, RET_IN)),
            _const_spec((2, SEQ // ROT_LO, RET_DK)),
            _const_spec((4, ROT_LO, RET_DK)),
            _const_spec(state_shape),
            _const_spec((RET_VDIM, D_MODEL)),
            _const_spec((SMALL_BLOCK, RET_IN), (1, 0)),
            _const_spec((2 * D_MODEL, DEC_BATCH)),
            samp_spec,
        ],
        out_specs=[
            pl.BlockSpec((chunk, D_MODEL), row_map),
            pl.BlockSpec((1,) + state_shape, lambda b, c: (b, 0, 0, 0)),
            pl.BlockSpec((DEC_BATCH, RET_VDIM), lambda b, c: (0, 0)),
            samp_spec,
        ],
        out_shape=[
            jax.ShapeDtypeStruct(x.shape, F32),
            jax.ShapeDtypeStruct((BATCH,) + state_shape, F32),
            jax.ShapeDtypeStruct((DEC_BATCH, RET_VDIM), F32),
            jax.ShapeDtypeStruct(state.shape, F32),
        ],
        scratch_shapes=[
            pltpu.VMEM((chunk, RET_VDIM), BF16),
            pltpu.VMEM((RET_HEADS, chunk, chunk), F32),
            pltpu.VMEM((RET_HEADS, chunk, LANES), F32),
            pltpu.VMEM((RET_HEADS, chunk, LANES), F32),
            pltpu.VMEM((chunk, RET_DK), F32),
            pltpu.VMEM((chunk, RET_DK), F32),
        ],
        compiler_params=pltpu.CompilerParams(
            dimension_semantics=("arbitrary", "arbitrary"),
            vmem_limit_bytes=RET_LAYER_VMEM_LIMIT),
        name="ret_layer",
    )(x, g, w_in_b, *tabs, s0, w_out_b, proj_small, qkt_small, state)


def _gn_outproj_kernel(o_ref, g_ref, x_ref, wo_ref, xo_ref):
    parts = []
    for h in range(RET_HEADS):
        sl = slice(h * RET_DV, (h + 1) * RET_DV)
        parts.append(_gn_gate(o_ref[:, sl], g_ref[:, sl]).astype(BF16))
    gated = jnp.concatenate(parts, axis=1)
    xo_ref[...] = x_ref[...] + jnp.dot(gated, wo_ref[...], preferred_element_type=F32)


def _gn_outproj(o, proj_small, x_samp, w_out_b):
    return pl.pallas_call(
        _gn_outproj_kernel,
        grid=(1,),
        in_specs=[
            _const_spec((DEC_BATCH, RET_VDIM)),
            _const_spec((SMALL_BLOCK, RET_VDIM), (1, 2)),
            _const_spec((DEC_BATCH, D_MODEL)),
            _const_spec((RET_VDIM, D_MODEL)),
        ],
        out_specs=pl.BlockSpec((DEC_BATCH, D_MODEL), lambda i: (0, 0)),
        out_shape=jax.ShapeDtypeStruct((DEC_BATCH, D_MODEL), F32),
        compiler_params=_params(1),
        name="gn_outproj",
    )(o, proj_small, x_samp, w_out_b)


def _n_parts(kind):
    return 2 if kind == "ffn" else 3


def _mlp_pre(parts, kind):
    if kind == "ffn":
        return parts[0], parts[1]
    return parts[1] * parts[2], parts[0]


def _mlp_act(pre, p1, p2, other, cw, cb, kind):
    conv = cw[0:1] * p2 + cw[1:2] * p1 + cw[2:3] * pre
    if kind == "ffn":
        a = conv + cb
        return (a / (1.0 + jnp.exp(-a))) * other
    return other * conv


def _mlp_layer_kernel(*refs, tm, dff, kind, final_norm, stacked):
    npart = _n_parts(kind)
    nb = SMALL_BLOCK
    nc = dff // MLP_CW
    tiles = SEQ // tm
    refs = list(refs)
    xm_ref, xsamp_ref, xb_ref, g_ref = refs[0:4]
    w_refs = refs[4:4 + npart]
    cw_ref, cb_ref, wo_ref, st_ref, gf_ref = refs[4 + npart:9 + npart]
    rest = refs[9 + npart:]
    if stacked:
        sn_prev_ref, stp_prev_ref = rest[0:2]
        rest = rest[2:]
    xso_ref, sn_ref, xbo_ref, stp_ref = rest[0:4]
    wb_scrs = rest[4:4 + npart]
    (wob_scr, h_scr, acc_scr, meta_carry_scr, carry_scr,
     pre_scr, act_scr) = rest[4 + npart:]
    sn_new = sn_ref.at[1] if stacked else sn_ref
    i = pl.program_id(0)

    @pl.when(i < nc)
    def _small_step():
        col = pl.multiple_of(i * MLP_CW, MLP_CW)
        cols = pl.ds(col, MLP_CW)

        @pl.when(i == 0)
        def _():
            h_scr[0:nb, :] = _rms(xm_ref[...], g_ref[...]).astype(BF16)
            h_scr[nb:, :] = _rms(xsamp_ref[...], g_ref[...]).astype(BF16)
            acc_scr[...] = jnp.zeros_like(acc_scr)

        parts = []
        for w_ref, wb_scr in zip(w_refs, wb_scrs):
            wb = w_ref[...].astype(BF16)
            wb_scr[:, cols] = wb
            parts.append(jnp.dot(h_scr[...], wb, preferred_element_type=F32))
        pre, other = _mlp_pre(parts, kind)
        cw = cw_ref[:, cols]
        cb = cb_ref[:, cols]
        pre_scr[0:SUBLANES, :] = jnp.zeros((SUBLANES, MLP_CW), F32)
        pre_scr[SUBLANES:SUBLANES + nb, :] = pre[0:nb]
        act_meta = _mlp_act(pre[0:nb], pre_scr[SUBLANES - 1:SUBLANES - 1 + nb, :],
                            pre_scr[SUBLANES - 2:SUBLANES - 2 + nb, :], other[0:nb],
                            cw, cb, kind)
        meta_carry_scr[:, cols] = pre[N_META - SUBLANES:N_META]
        s0 = st_ref[:, 0, :]
        s1 = st_ref[:, 1, :]
        sn_new[:, 0, :] = s1
        sn_new[:, 1, :] = pre[nb:]
        if stacked:
            sn_ref[0] = sn_prev_ref[...]
        act_samp = _mlp_act(pre[nb:], s1, s0, other[nb:], cw, cb, kind)
        act = jnp.concatenate([act_meta, act_samp], axis=0).astype(BF16)
        wob = wo_ref[...].astype(BF16)
        wob_scr[cols, :] = wob
        acc_scr[...] += jnp.dot(act, wob, preferred_element_type=F32)

        @pl.when(i == nc - 1)
        def _():
            for x_ref, rows in ((xm_ref, slice(0, nb)), (xsamp_ref, slice(nb, 2 * nb))):
                out = x_ref[...] + acc_scr[rows, :]
                if final_norm:
                    out = _rms(out, gf_ref[...])
                xso_ref[rows, :] = out

    @pl.when(i >= nc)
    def _big_step():
        @pl.when((i - nc) % tiles == 0)
        def _():
            carry_scr[...] = meta_carry_scr[...]

        h = _rms(xb_ref[...], g_ref[...]).astype(BF16)
        for c0 in range(0, dff, MLP_CW):
            sl = slice(c0, c0 + MLP_CW)
            parts = [jnp.dot(h, w[:, sl], preferred_element_type=F32) for w in wb_scrs]
            pre, other = _mlp_pre(parts, kind)
            pre_scr[0:SUBLANES, :] = carry_scr[:, sl]
            pre_scr[SUBLANES:SUBLANES + tm, :] = pre
            p1 = pre_scr[SUBLANES - 1:SUBLANES - 1 + tm, :]
            p2 = pre_scr[SUBLANES - 2:SUBLANES - 2 + tm, :]
            act_scr[:, sl] = _mlp_act(pre, p1, p2, other, cw_ref[:, sl], cb_ref[:, sl],
                                      kind).astype(BF16)
            carry_scr[:, sl] = pre[tm - SUBLANES:tm, :]
        out = xb_ref[...] + jnp.dot(act_scr[...], wob_scr[...], preferred_element_type=F32)
        if final_norm:
            out = _rms(out, gf_ref[...])
        xbo_ref[...] = out
        if stacked:
            stp_ref[0] = stp_prev_ref[...]
            stp_ref[1, 0] = carry_scr[SUBLANES - 2:SUBLANES, :]
        else:
            stp_ref[0] = carry_scr[SUBLANES - 2:SUBLANES, :]


def _mlp_layer(x_meta, x_samp, xb, g, g_layer, w_in, conv_w, conv_b, w_out, state, layer, g_final,
               *, kind, final_norm, prev_states=None):
    npart = _n_parts(kind)
    dff = w_out.shape[1]
    nc = dff // MLP_CW
    stacked = prev_states is not None
    tm = 512 if stacked else 1024
    tiles = SEQ // tm
    chunk = lambda i: jnp.minimum(i, nc - 1)
    tile = lambda i: jnp.maximum(i - nc, 0)
    half_spec = lambda blk: _const_spec((SMALL_BLOCK, D_MODEL), (blk, 0))
    lead = (2,) if stacked else ()
    zero = (0,) if stacked else ()
    prev_specs = [
        pl.BlockSpec((DEC_BATCH, 2, MLP_CW), lambda i: (0, 0, chunk(i))),
        pl.BlockSpec((1, 2, dff), lambda i: (tile(i) // tiles, 0, 0)),
    ] if stacked else []
    return pl.pallas_call(
        functools.partial(_mlp_layer_kernel, tm=tm, dff=dff, kind=kind, final_norm=final_norm,
                          stacked=stacked),
        grid=(nc + BATCH * tiles,),
        in_specs=[
            half_spec(x_meta[1]),
            half_spec(x_samp[1]),
            pl.BlockSpec((tm, D_MODEL), lambda i: (tile(i), 0)),
            _const_spec((None, 1, D_MODEL), (g_layer, 0, 0)),
            *[pl.BlockSpec((None, D_MODEL, MLP_CW), functools.partial(
                lambda i, p: (layer, 0, p * nc + chunk(i)), p=p)) for p in range(npart)],
            _const_spec((None, 3, dff), (layer, 0, 0)),
            _const_spec((None, 1, dff), (layer, 0, 0)),
            pl.BlockSpec((None, MLP_CW, D_MODEL), lambda i: (layer, chunk(i), 0)),
            pl.BlockSpec((None, DEC_BATCH, 2, MLP_CW), lambda i: (layer, 0, 0, chunk(i))),
            _const_spec((None, 1, D_MODEL), (0, 0, 0)),
            *prev_specs,
        ],
        out_specs=[
            pl.BlockSpec((SMALL_ROWS, D_MODEL), lambda i: (0, 0)),
            pl.BlockSpec(lead + (DEC_BATCH, 2, MLP_CW), lambda i: zero + (0, 0, chunk(i))),
            pl.BlockSpec((tm, D_MODEL), lambda i: (tile(i), 0)),
            pl.BlockSpec(lead + (1, 2, dff), lambda i: zero + (tile(i) // tiles, 0, 0)),
        ],
        out_shape=[
            jax.ShapeDtypeStruct((SMALL_ROWS, D_MODEL), F32),
            jax.ShapeDtypeStruct(lead + (DEC_BATCH, 2, dff), F32),
            jax.ShapeDtypeStruct(xb.shape, F32),
            jax.ShapeDtypeStruct(lead + (BATCH, 2, dff), F32),
        ],
        scratch_shapes=[
            *[pltpu.VMEM((D_MODEL, dff), BF16) for _ in range(npart)],
            pltpu.VMEM((dff, D_MODEL), BF16),
            pltpu.VMEM((SMALL_ROWS, D_MODEL), BF16),
            pltpu.VMEM((SMALL_ROWS, D_MODEL), F32),
            pltpu.VMEM((SUBLANES, dff), F32),
            pltpu.VMEM((SUBLANES, dff), F32),
            pltpu.VMEM((tm + SUBLANES, MLP_CW), F32),
            pltpu.VMEM((tm, dff), BF16),
        ],
        compiler_params=_params(1),
        name="mlp_layer_" + kind,
    )(x_meta[0], x_samp[0], xb, g, *([w_in] * npart), conv_w, conv_b, w_out, state, g_final,
      *(prev_states or ()))


def _rope_angles(pos):
    half = RET_DK // 2
    inv = 1.0 / (ROPE_BASE ** jnp.linspace(0.0, 1.0, half, dtype=F32))
    return pos.astype(F32)[:, None] * jnp.repeat(inv, 2)[None, :]


def _rope_sign():
    return jnp.where(jnp.arange(RET_DK) % 2 == 0, -1.0, 1.0).astype(F32)[None, :]


def _rope_tables(pos):
    ang = _rope_angles(pos)
    return jnp.cos(ang), jnp.sin(ang) * _rope_sign()


def _rope_tables_split(first, count):
    coarse = _rope_angles(first + ROT_LO * jnp.arange(count // ROT_LO, dtype=jnp.int32))
    fine = _rope_angles(jnp.arange(ROT_LO, dtype=jnp.int32))
    sign = _rope_sign()
    cf, sf = jnp.cos(fine), jnp.sin(fine)
    return (jnp.stack([jnp.cos(coarse), jnp.sin(coarse)]),
            jnp.stack([cf, sf, sign * cf, sign * sf]))


def kernel(x_prompt, x_sample, state_ret, state_conv, state_ffn, meta_tokens, norm_mix, norm_ffn,
           norm_final, w_ret_in, w_ret_out, w_sc_in, w_sc_conv, w_sc_out, w_ffn_in, w_ffn_conv,
           b_ffn_conv, w_ffn_out):
    nb = SMALL_BLOCK
    xb = x_prompt.reshape(BATCH * SEQ, D_MODEL)
    x_samp = x_sample.reshape(DEC_BATCH, D_MODEL)

    g_mix = norm_mix.reshape(2, 1, D_MODEL)
    g_ffn = norm_ffn.reshape(2, 1, D_MODEL)
    g_final = norm_final.reshape(1, 1, D_MODEL)
    b_ffn = b_ffn_conv.reshape(2, 1, D_FF)

    pos_small = jnp.concatenate([jnp.arange(nb, dtype=jnp.int32),
                                 jnp.full((nb,), PAST_LEN, jnp.int32)])
    tabs_big = _rope_tables_split(N_META, SEQ)
    tabs_small = _rope_tables(pos_small)

    proj_s, w_ret_in_b, qkt_s = _ret_inproj_small(meta_tokens, x_samp, g_mix, w_ret_in, tabs_small)
    xs_meta, s_meta, w_ret_out_b = _ret_meta(proj_s, meta_tokens, w_ret_out)
    xb, ret_p, o_s, ret_s = _ret_layer(xb, g_mix, w_ret_in_b, tabs_big, s_meta, w_ret_out_b,
                                       proj_s, qkt_s, state_ret, chunk=256)
    xs_samp = _gn_outproj(o_s, proj_s, x_samp, w_ret_out_b)

    xs, ffn_s0, xb, ffn_p0 = _mlp_layer(
        (xs_meta, 0), (xs_samp, 0), xb, g_ffn, 0, w_ffn_in, w_ffn_conv, b_ffn, w_ffn_out,
        state_ffn, 0, g_final, kind="ffn", final_norm=False)

    zero_b = jnp.zeros((1, 1, D_MODEL), F32)
    xs, conv_s, xb, conv_p = _mlp_layer(
        (xs, 0), (xs, 1), xb, g_mix, 1, w_sc_in, w_sc_conv, zero_b, w_sc_out,
        state_conv, 0, g_final, kind="sc", final_norm=False)

    xs, ffn_s, xb, ffn_p = _mlp_layer(
        (xs, 0), (xs, 1), xb, g_ffn, 1, w_ffn_in, w_ffn_conv, b_ffn, w_ffn_out,
        state_ffn, 1, g_final, kind="ffn", final_norm=True, prev_states=(ffn_s0, ffn_p0))

    y_prompt = xb.reshape(BATCH, SEQ, D_MODEL)
    y_sample = xs[nb:].reshape(DEC_BATCH, 1, D_MODEL)
    return (y_prompt, y_sample, ret_p[None], ret_s, conv_p[None], conv_s[None], ffn_p, ffn_s)
```

```python
import functools
import math

import jax
import jax.numpy as jnp
from jax import lax
from jax.experimental import pallas as pl
from jax.experimental.pallas import tpu as pltpu

F32 = jnp.float32
BF16 = jnp.bfloat16

D_MODEL = 1024
SEQ = 2048
BATCH = 8
DEC_BATCH = 128
PAST_LEN = 16384
N_META = 16
RET_HEADS = 4
RET_DK = 256
RET_DV = 512
RET_VDIM = RET_HEADS * RET_DV
RET_IN = 2 * D_MODEL + 2 * RET_VDIM
ROPE_BASE = 10000.0
D_FF = 2816
NORM_EPS = 1e-6
GN_EPS = 1e-6

LANES = 128
SUBLANES = 8
SMALL_BLOCK = 128
SMALL_ROWS = 2 * SMALL_BLOCK
MLP_CW = 256
PROJ_CW = 512
ROT_LO = 64
VMEM_LIMIT = 60 * 1024 * 1024


def _params(n_axes):
    return pltpu.CompilerParams(
        dimension_semantics=("arbitrary",) * n_axes,
        vmem_limit_bytes=VMEM_LIMIT)


def _const_spec(shape, index=None):
    index = (0,) * len(shape) if index is None else index
    return pl.BlockSpec(shape, lambda *_: index, pipeline_mode=pl.Buffered(1))


def _rms(x, g):
    ms = jnp.mean(x * x, axis=-1, keepdims=True)
    return x * lax.rsqrt(ms + NORM_EPS) * g


def _rotary(zs, col, c_ref, s_ref):
    sl = slice(col, col + LANES)
    even = lax.broadcasted_iota(jnp.int32, zs.shape, 1) % 2 == 0
    partner = jnp.where(even, pltpu.roll(zs, LANES - 1, 1), pltpu.roll(zs, 1, 1))
    return zs * c_ref[:, sl] + partner * s_ref[:, sl]


def _sample_state_update(p_ref, qkt_ref, s_ref, o_ref, sn_ref, first, count):
    tile = RET_DV // LANES
    for j in range(count):
        b = first + j
        row = p_ref[pl.ds(b, 1), :]
        lane_b = jnp.full((RET_DK, LANES), b, jnp.int32)
        for h in range(RET_HEADS):
            gamma = 1.0 - 2.0 ** (-5.0 - h)
            qs, ks, vs, _ = _head_cols(h)
            v = row[:, vs]
            q_rows = jnp.take_along_axis(qkt_ref[qs, :], lane_b, axis=1)
            k_rows = jnp.take_along_axis(qkt_ref[ks, :], lane_b, axis=1)
            s_prev = s_ref[j, h]
            cross = jnp.sum(s_prev * _lane_tile(q_rows, tile), axis=0, keepdims=True)
            qk = jnp.sum(row[:, qs] * row[:, ks], axis=-1, keepdims=True)
            o_ref[pl.ds(b, 1), h * RET_DV:(h + 1) * RET_DV] = qk * v + gamma * cross
            sn_ref[j, h] = gamma * s_prev + _lane_tile(k_rows, tile) * v


QK_CHUNKS = 2 * D_MODEL // PROJ_CW


def _ret_inproj_small_kernel(xm_ref, xs_ref, g_ref, w_ref, c_ref, s_ref,
                             o_ref, wb_ref, qkt_ref, h_scr):
    c = pl.program_id(0)

    @pl.when(c == 0)
    def _():
        h_scr[0:N_META, :] = _rms(xm_ref[...], g_ref[...]).astype(BF16)
        h_scr[N_META:SMALL_BLOCK, :] = jnp.zeros((SMALL_BLOCK - N_META, D_MODEL), BF16)
        h_scr[SMALL_BLOCK:, :] = _rms(xs_ref[...], g_ref[...]).astype(BF16)

    wb = w_ref[...].astype(BF16)
    wb_ref[...] = wb
    z = jnp.dot(h_scr[...], wb, preferred_element_type=F32)

    @pl.when(c >= QK_CHUNKS)
    def _():
        o_ref[...] = z

    @pl.when(c < QK_CHUNKS)
    def _():
        scale = jnp.where(c >= QK_CHUNKS // 2, RET_DK ** -0.5, 1.0)
        for s0 in range(0, PROJ_CW, LANES):
            r = _rotary(z[:, s0:s0 + LANES], s0 % RET_DK, c_ref, s_ref) * scale
            o_ref[:, s0:s0 + LANES] = r
            qkt_ref[s0:s0 + LANES, :] = r[SMALL_BLOCK:].T


def _ret_inproj_small(x_meta, x_samp, g, w, tabs):
    return pl.pallas_call(
        _ret_inproj_small_kernel,
        grid=(RET_IN // PROJ_CW,),
        in_specs=[
            _const_spec((N_META, D_MODEL)),
            _const_spec((DEC_BATCH, D_MODEL)),
            _const_spec((None, 1, D_MODEL), (0, 0, 0)),
            pl.BlockSpec((None, D_MODEL, PROJ_CW), lambda c: (0, 0, c)),
            _const_spec((SMALL_ROWS, RET_DK)),
            _const_spec((SMALL_ROWS, RET_DK)),
        ],
        out_specs=[
            pl.BlockSpec((SMALL_ROWS, PROJ_CW), lambda c: (0, c)),
            pl.BlockSpec((D_MODEL, PROJ_CW), lambda c: (0, c)),
            pl.BlockSpec((PROJ_CW, DEC_BATCH), lambda c: (jnp.minimum(c, QK_CHUNKS - 1), 0)),
        ],
        out_shape=[
            jax.ShapeDtypeStruct((SMALL_ROWS, RET_IN), F32),
            jax.ShapeDtypeStruct((D_MODEL, RET_IN), BF16),
            jax.ShapeDtypeStruct((2 * D_MODEL, DEC_BATCH), F32),
        ],
        scratch_shapes=[pltpu.VMEM((SMALL_ROWS, D_MODEL), BF16)],
        compiler_params=_params(1),
        name="ret_inproj_small",
    )(x_meta, x_samp, g, w, *tabs)


def _gn_gate(o, g):
    mu = jnp.mean(o, axis=-1, keepdims=True)
    d = o - mu
    var = jnp.mean(d * d, axis=-1, keepdims=True)
    on = d * lax.rsqrt(var + GN_EPS)
    gf = g.astype(F32)
    return (gf / (1.0 + jnp.exp(-gf))) * on


def _lane_tile(x, n):
    return jnp.concatenate([x] * n, axis=1)


def _head_cols(h):
    q0 = h * RET_DK
    k0 = D_MODEL + h * RET_DK
    v0 = 2 * D_MODEL + h * RET_DV
    g0 = 2 * D_MODEL + RET_VDIM + h * RET_DV
    return (slice(q0, q0 + RET_DK), slice(k0, k0 + RET_DK),
            slice(v0, v0 + RET_DV), slice(g0, g0 + RET_DV))


def _decay_tables(decay_scr, cdec_scr, kdec_scr, chunk, valid):
    n_mat = lax.broadcasted_iota(jnp.int32, (chunk, chunk), 0)
    m_mat = lax.broadcasted_iota(jnp.int32, (chunk, chunk), 1)
    diff = (n_mat - m_mat).astype(F32)
    causal = diff >= 0
    diff = jnp.where(causal, diff, 0.0)
    n_col = lax.broadcasted_iota(jnp.int32, (chunk, LANES), 0).astype(F32)
    for h in range(RET_HEADS):
        lg = math.log(1.0 - 2.0 ** (-5.0 - h))
        decay_scr[h] = jnp.where(causal, jnp.exp(lg * diff), 0.0)
        cdec_scr[h] = jnp.exp(lg * (n_col + 1.0))
        kdec_scr[h] = jnp.where(n_col < valid, jnp.exp(lg * (valid - 1.0 - n_col)), 0.0)


def _ret_head(h, q, k, v, g, s_ref, decay_scr, cdec_scr, kdec_scr, gated_scr, valid):
    lg = math.log(1.0 - 2.0 ** (-5.0 - h))
    s_prev = s_ref[...]
    scores = lax.dot_general(q, k.astype(BF16), (((1,), (1,)), ((), ())),
                             preferred_element_type=F32)
    inner = jnp.dot((scores * decay_scr[h]).astype(BF16), v, preferred_element_type=F32)
    cross = jnp.dot(q, s_prev.astype(BF16), preferred_element_type=F32)
    o = inner + cross * _lane_tile(cdec_scr[h], RET_DV // LANES)
    kd = (k * _lane_tile(kdec_scr[h], RET_DK // LANES)).astype(BF16)
    kv = lax.dot_general(kd, v, (((0,), (0,)), ((), ())), preferred_element_type=F32)
    s_ref[...] = math.exp(lg * valid) * s_prev + kv
    gated_scr[:, h * RET_DV:(h + 1) * RET_DV] = _gn_gate(o, g).astype(BF16)


def _ret_meta_kernel(p_ref, x_ref, wo_ref, xo_ref, sf_ref, wob_ref,
                     gated_scr, decay_scr, cdec_scr, kdec_scr):
    _decay_tables(decay_scr, cdec_scr, kdec_scr, SMALL_BLOCK, N_META)
    sf_ref[...] = jnp.zeros_like(sf_ref)
    for h in range(RET_HEADS):
        qs, ks, vs, gs = _head_cols(h)
        _ret_head(h, p_ref[:, qs].astype(BF16), p_ref[:, ks], p_ref[:, vs].astype(BF16),
                  p_ref[:, gs], sf_ref.at[h], decay_scr, cdec_scr, kdec_scr, gated_scr, N_META)
    wo = wo_ref[...].astype(BF16)
    wob_ref[...] = wo
    out = jnp.dot(gated_scr[...], wo, preferred_element_type=F32)
    xo_ref[0:N_META, :] = x_ref[...] + out[0:N_META]
    xo_ref[N_META:, :] = out[N_META:]


def _ret_meta(proj_small, x_meta, w_out):
    nb = SMALL_BLOCK
    state_shape = (RET_HEADS, RET_DK, RET_DV)
    w_shape = (RET_VDIM, D_MODEL)
    return pl.pallas_call(
        _ret_meta_kernel,
        grid=(1,),
        in_specs=[
            _const_spec((nb, RET_IN)),
            _const_spec((N_META, D_MODEL)),
            _const_spec((None,) + w_shape, (0, 0, 0)),
        ],
        out_specs=[
            pl.BlockSpec((nb, D_MODEL), lambda i: (0, 0)),
            pl.BlockSpec(state_shape, lambda i: (0, 0, 0)),
            pl.BlockSpec(w_shape, lambda i: (0, 0)),
        ],
        out_shape=[
            jax.ShapeDtypeStruct((nb, D_MODEL), F32),
            jax.ShapeDtypeStruct(state_shape, F32),
            jax.ShapeDtypeStruct(w_shape, BF16),
        ],
        scratch_shapes=[
            pltpu.VMEM((nb, RET_VDIM), BF16),
            pltpu.VMEM((RET_HEADS, nb, nb), F32),
            pltpu.VMEM((RET_HEADS, nb, LANES), F32),
            pltpu.VMEM((RET_HEADS, nb, LANES), F32),
        ],
        compiler_params=_params(1),
        name="ret_meta",
    )(proj_small, x_meta, w_out)


def _ret_layer_kernel(x_ref, g_ref, w_ref, ta_ref, tb_ref, s0_ref, wo_ref, ps_ref,
                      qkt_ref, st_ref, xo_ref, sf_ref, os_ref, sn_ref,
                      gated_scr, decay_scr, cdec_scr, kdec_scr, c_ref, s_ref, *, chunk, bb):
    b = pl.program_id(0)
    c = pl.program_id(1)

    @pl.when((b == 0) & (c == 0))
    def _():
        _decay_tables(decay_scr, cdec_scr, kdec_scr, chunk, chunk)

    for blk in range(chunk // ROT_LO):
        rows = slice(blk * ROT_LO, (blk + 1) * ROT_LO)
        hi = pl.ds(c * (chunk // ROT_LO) + blk, 1)
        ca, sa = ta_ref[0, hi, :], ta_ref[1, hi, :]
        c_ref[rows, :] = ca * tb_ref[0] - sa * tb_ref[1]
        s_ref[rows, :] = sa * tb_ref[2] + ca * tb_ref[3]

    @pl.when(c == 0)
    def _():
        sf_ref[0] = s0_ref[...]

    step = b * pl.num_programs(1) + c
    _sample_state_update(ps_ref, qkt_ref, st_ref, os_ref, sn_ref, step * bb, bb)

    hn = _rms(x_ref[...], g_ref[...]).astype(BF16)
    proj = lambda sl: jnp.dot(hn, w_ref[:, sl], preferred_element_type=F32)
    rot = lambda z: jnp.concatenate(
        [_rotary(z[:, s0:s0 + LANES], s0, c_ref, s_ref)
         for s0 in range(0, RET_DK, LANES)], axis=1)
    for h in range(RET_HEADS):
        qs, ks, vs, gs = _head_cols(h)
        q = rot(proj(qs)).astype(BF16)
        k = rot(proj(ks)) * (RET_DK ** -0.5)
        _ret_head(h, q, k, proj(vs).astype(BF16), proj(gs), sf_ref.at[0, h],
                  decay_scr, cdec_scr, kdec_scr, gated_scr, chunk)
    xo_ref[...] = x_ref[...] + jnp.dot(gated_scr[...], wo_ref[...],
                                       preferred_element_type=F32)


def _ret_layer(x, g, w_in_b, tabs, s0, w_out_b, proj_small, qkt_small, state, *, chunk):
    n_chunks = SEQ // chunk
    steps = BATCH * n_chunks
    bb = DEC_BATCH // steps
    row_map = lambda b, c: (b * n_chunks + c, 0)
    state_shape = (RET_HEADS, RET_DK, RET_DV)
    samp_spec = pl.BlockSpec((None, bb) + state_shape,
                             lambda b, c: (0, b * n_chunks + c, 0, 0, 0))
    return pl.pallas_call(
        functools.partial(_ret_layer_kernel, chunk=chunk, bb=bb),
        grid=(BATCH, n_chunks),
        in_specs=[
            pl.BlockSpec((chunk, D_MODEL), row_map),
            _const_spec((None, 1, D_MODEL), (0, 0, 0)),
            _const_spec((D_MODEL, RET_IN)),
            _const_spec((2, SEQ // ROT_LO, RET_DK)),
            _const_spec((4, ROT_LO, RET_DK)),
            _const_spec(state_shape),
            _const_spec((RET_VDIM, D_MODEL)),
            _const_spec((SMALL_BLOCK, RET_IN), (1, 0)),
            _const_spec((2 * D_MODEL, DEC_BATCH)),
            samp_spec,
        ],
        out_specs=[
            pl.BlockSpec((chunk, D_MODEL), row_map),
            pl.BlockSpec((1,) + state_shape, lambda b, c: (b, 0, 0, 0)),
            pl.BlockSpec((DEC_BATCH, RET_VDIM), lambda b, c: (0, 0)),
            samp_spec,
        ],
        out_shape=[
            jax.ShapeDtypeStruct(x.shape, F32),
            jax.ShapeDtypeStruct((BATCH,) + state_shape, F32),
            jax.ShapeDtypeStruct((DEC_BATCH, RET_VDIM), F32),
            jax.ShapeDtypeStruct(state.shape, F32),
        ],
        scratch_shapes=[
            pltpu.VMEM((chunk, RET_VDIM), BF16),
            pltpu.VMEM((RET_HEADS, chunk, chunk), F32),
            pltpu.VMEM((RET_HEADS, chunk, LANES), F32),
            pltpu.VMEM((RET_HEADS, chunk, LANES), F32),
            pltpu.VMEM((chunk, RET_DK), F32),
            pltpu.VMEM((chunk, RET_DK), F32),
        ],
        compiler_params=_params(2),
        name="ret_layer",
    )(x, g, w_in_b, *tabs, s0, w_out_b, proj_small, qkt_small, state)


def _gn_outproj_kernel(o_ref, g_ref, x_ref, wo_ref, xo_ref):
    parts = []
    for h in range(RET_HEADS):
        sl = slice(h * RET_DV, (h + 1) * RET_DV)
        parts.append(_gn_gate(o_ref[:, sl], g_ref[:, sl]).astype(BF16))
    gated = jnp.concatenate(parts, axis=1)
    xo_ref[...] = x_ref[...] + jnp.dot(gated, wo_ref[...], preferred_element_type=F32)


def _gn_outproj(o, proj_small, x_samp, w_out_b):
    return pl.pallas_call(
        _gn_outproj_kernel,
        grid=(1,),
        in_specs=[
            _const_spec((DEC_BATCH, RET_VDIM)),
            _const_spec((SMALL_BLOCK, RET_VDIM), (1, 2)),
            _const_spec((DEC_BATCH, D_MODEL)),
            _const_spec((RET_VDIM, D_MODEL)),
        ],
        out_specs=pl.BlockSpec((DEC_BATCH, D_MODEL), lambda i: (0, 0)),
        out_shape=jax.ShapeDtypeStruct((DEC_BATCH, D_MODEL), F32),
        compiler_params=_params(1),
        name="gn_outproj",
    )(o, proj_small, x_samp, w_out_b)


def _n_parts(kind):
    return 2 if kind == "ffn" else 3


def _mlp_pre(parts, kind):
    if kind == "ffn":
        return parts[0], parts[1]
    return parts[1] * parts[2], parts[0]


def _mlp_act(pre, p1, p2, other, cw, cb, kind):
    conv = cw[0:1] * p2 + cw[1:2] * p1 + cw[2:3] * pre
    if kind == "ffn":
        a = conv + cb
        return (a / (1.0 + jnp.exp(-a))) * other
    return other * conv


def _mlp_layer_kernel(*refs, tm, dff, kind, final_norm, stacked):
    npart = _n_parts(kind)
    nb = SMALL_BLOCK
    nc = dff // MLP_CW
    tiles = SEQ // tm
    refs = list(refs)
    xm_ref, xsamp_ref, xb_ref, g_ref = refs[0:4]
    w_refs = refs[4:4 + npart]
    cw_ref, cb_ref, wo_ref, st_ref, gf_ref = refs[4 + npart:9 + npart]
    rest = refs[9 + npart:]
    if stacked:
        sn_prev_ref, stp_prev_ref = rest[0:2]
        rest = rest[2:]
    xso_ref, sn_ref, xbo_ref, stp_ref = rest[0:4]
    wb_scrs = rest[4:4 + npart]
    (wob_scr, h_scr, acc_scr, meta_carry_scr, carry_scr,
     pre_scr, act_scr) = rest[4 + npart:]
    sn_new = sn_ref.at[1] if stacked else sn_ref
    i = pl.program_id(0)

    @pl.when(i < nc)
    def _small_step():
        col = pl.multiple_of(i * MLP_CW, MLP_CW)
        cols = pl.ds(col, MLP_CW)

        @pl.when(i == 0)
        def _():
            h_scr[0:nb, :] = _rms(xm_ref[...], g_ref[...]).astype(BF16)
            h_scr[nb:, :] = _rms(xsamp_ref[...], g_ref[...]).astype(BF16)
            acc_scr[...] = jnp.zeros_like(acc_scr)

        parts = []
        for w_ref, wb_scr in zip(w_refs, wb_scrs):
            wb = w_ref[...].astype(BF16)
            wb_scr[:, cols] = wb
            parts.append(jnp.dot(h_scr[...], wb, preferred_element_type=F32))
        pre, other = _mlp_pre(parts, kind)
        cw = cw_ref[:, cols]
        cb = cb_ref[:, cols]
        pre_scr[0:SUBLANES, :] = jnp.zeros((SUBLANES, MLP_CW), F32)
        pre_scr[SUBLANES:SUBLANES + nb, :] = pre[0:nb]
        act_meta = _mlp_act(pre[0:nb], pre_scr[SUBLANES - 1:SUBLANES - 1 + nb, :],
                            pre_scr[SUBLANES - 2:SUBLANES - 2 + nb, :], other[0:nb],
                            cw, cb, kind)
        meta_carry_scr[:, cols] = pre[N_META - SUBLANES:N_META]
        s0 = st_ref[:, 0, :]
        s1 = st_ref[:, 1, :]
        sn_new[:, 0, :] = s1
        sn_new[:, 1, :] = pre[nb:]
        if stacked:
            sn_ref[0] = sn_prev_ref[...]
        act_samp = _mlp_act(pre[nb:], s1, s0, other[nb:], cw, cb, kind)
        act = jnp.concatenate([act_meta, act_samp], axis=0).astype(BF16)
        wob = wo_ref[...].astype(BF16)
        wob_scr[cols, :] = wob
        acc_scr[...] += jnp.dot(act, wob, preferred_element_type=F32)

        @pl.when(i == nc - 1)
        def _():
            for x_ref, rows in ((xm_ref, slice(0, nb)), (xsamp_ref, slice(nb, 2 * nb))):
                out = x_ref[...] + acc_scr[rows, :]
                if final_norm:
                    out = _rms(out, gf_ref[...])
                xso_ref[rows, :] = out

    @pl.when(i >= nc)
    def _big_step():
        @pl.when((i - nc) % tiles == 0)
        def _():
            carry_scr[...] = meta_carry_scr[...]

        h = _rms(xb_ref[...], g_ref[...]).astype(BF16)
        for c0 in range(0, dff, MLP_CW):
            sl = slice(c0, c0 + MLP_CW)
            parts = [jnp.dot(h, w[:, sl], preferred_element_type=F32) for w in wb_scrs]
            pre, other = _mlp_pre(parts, kind)
            head_rows = lax.broadcasted_iota(jnp.int32, (SUBLANES, MLP_CW), 0)
            prev = carry_scr[:, sl]
            shifted = []
            for d in (1, 2):
                rolled = pltpu.roll(pre, d, 0)
                first = jnp.where(head_rows < d, pltpu.roll(prev, d, 0), rolled[0:SUBLANES])
                shifted.append(jnp.concatenate([first, rolled[SUBLANES:]], axis=0))
            act_scr[:, sl] = _mlp_act(pre, shifted[0], shifted[1], other, cw_ref[:, sl],
                                      cb_ref[:, sl], kind).astype(BF16)
            carry_scr[:, sl] = pre[tm - SUBLANES:tm, :]
        out = xb_ref[...] + jnp.dot(act_scr[...], wob_scr[...], preferred_element_type=F32)
        if final_norm:
            out = _rms(out, gf_ref[...])
        xbo_ref[...] = out
        if stacked:
            stp_ref[0] = stp_prev_ref[...]
            stp_ref[1, 0] = carry_scr[SUBLANES - 2:SUBLANES, :]
        else:
            stp_ref[0] = carry_scr[SUBLANES - 2:SUBLANES, :]


def _mlp_layer(x_meta, x_samp, xb, g, g_layer, w_in, conv_w, conv_b, w_out, state, layer, g_final,
               *, kind, final_norm, prev_states=None):
    npart = _n_parts(kind)
    dff = w_out.shape[1]
    nc = dff // MLP_CW
    stacked = prev_states is not None
    tm = 1024
    tiles = SEQ // tm
    chunk = lambda i: jnp.minimum(i, nc - 1)
    tile = lambda i: jnp.maximum(i - nc, 0)
    half_spec = lambda blk: _const_spec((SMALL_BLOCK, D_MODEL), (blk, 0))
    lead = (2,) if stacked else ()
    zero = (0,) if stacked else ()
    prev_specs = [
        pl.BlockSpec((DEC_BATCH, 2, MLP_CW), lambda i: (0, 0, chunk(i))),
        pl.BlockSpec((1, 2, dff), lambda i: (tile(i) // tiles, 0, 0)),
    ] if stacked else []
    return pl.pallas_call(
        functools.partial(_mlp_layer_kernel, tm=tm, dff=dff, kind=kind, final_norm=final_norm,
                          stacked=stacked),
        grid=(nc + BATCH * tiles,),
        in_specs=[
            half_spec(x_meta[1]),
            half_spec(x_samp[1]),
            pl.BlockSpec((tm, D_MODEL), lambda i: (tile(i), 0)),
            _const_spec((None, 1, D_MODEL), (g_layer, 0, 0)),
            *[pl.BlockSpec((None, D_MODEL, MLP_CW), functools.partial(
                lambda i, p: (layer, 0, p * nc + chunk(i)), p=p)) for p in range(npart)],
            _const_spec((None, 3, dff), (layer, 0, 0)),
            _const_spec((None, 1, dff), (layer, 0, 0)),
            pl.BlockSpec((None, MLP_CW, D_MODEL), lambda i: (layer, chunk(i), 0)),
            pl.BlockSpec((None, DEC_BATCH, 2, MLP_CW), lambda i: (layer, 0, 0, chunk(i))),
            _const_spec((None, 1, D_MODEL), (0, 0, 0)),
            *prev_specs,
        ],
        out_specs=[
            pl.BlockSpec((SMALL_ROWS, D_MODEL), lambda i: (0, 0)),
            pl.BlockSpec(lead + (DEC_BATCH, 2, MLP_CW), lambda i: zero + (0, 0, chunk(i))),
            pl.BlockSpec((tm, D_MODEL), lambda i: (tile(i), 0)),
            pl.BlockSpec(lead + (1, 2, dff), lambda i: zero + (tile(i) // tiles, 0, 0)),
        ],
        out_shape=[
            jax.ShapeDtypeStruct((SMALL_ROWS, D_MODEL), F32),
            jax.ShapeDtypeStruct(lead + (DEC_BATCH, 2, dff), F32),
            jax.ShapeDtypeStruct(xb.shape, F32),
            jax.ShapeDtypeStruct(lead + (BATCH, 2, dff), F32),
        ],
        scratch_shapes=[
            *[pltpu.VMEM((D_MODEL, dff), BF16) for _ in range(npart)],
            pltpu.VMEM((dff, D_MODEL), BF16),
            pltpu.VMEM((SMALL_ROWS, D_MODEL), BF16),
            pltpu.VMEM((SMALL_ROWS, D_MODEL), F32),
            pltpu.VMEM((SUBLANES, dff), F32),
            pltpu.VMEM((SUBLANES, dff), F32),
            pltpu.VMEM((SMALL_BLOCK + SUBLANES, MLP_CW), F32),
            pltpu.VMEM((tm, dff), BF16),
        ],
        compiler_params=_params(1),
        name="mlp_layer_" + kind,
    )(x_meta[0], x_samp[0], xb, g, *([w_in] * npart), conv_w, conv_b, w_out, state, g_final,
      *(prev_states or ()))


def _rope_angles(pos):
    half = RET_DK // 2
    inv = 1.0 / (ROPE_BASE ** jnp.linspace(0.0, 1.0, half, dtype=F32))
    return pos.astype(F32)[:, None] * jnp.repeat(inv, 2)[None, :]


def _rope_sign():
    return jnp.where(jnp.arange(RET_DK) % 2 == 0, -1.0, 1.0).astype(F32)[None, :]


def _rope_tables(pos):
    ang = _rope_angles(pos)
    return jnp.cos(ang), jnp.sin(ang) * _rope_sign()


def _rope_tables_split(first, count):
    coarse = _rope_angles(first + ROT_LO * jnp.arange(count // ROT_LO, dtype=jnp.int32))
    fine = _rope_angles(jnp.arange(ROT_LO, dtype=jnp.int32))
    sign = _rope_sign()
    cf, sf = jnp.cos(fine), jnp.sin(fine)
    return (jnp.stack([jnp.cos(coarse), jnp.sin(coarse)]),
            jnp.stack([cf, sf, sign * cf, sign * sf]))


def kernel(x_prompt, x_sample, state_ret, state_conv, state_ffn, meta_tokens, norm_mix, norm_ffn,
           norm_final, w_ret_in, w_ret_out, w_sc_in, w_sc_conv, w_sc_out, w_ffn_in, w_ffn_conv,
           b_ffn_conv, w_ffn_out):
    nb = SMALL_BLOCK
    xb = x_prompt.reshape(BATCH * SEQ, D_MODEL)
    x_samp = x_sample.reshape(DEC_BATCH, D_MODEL)

    g_mix = norm_mix.reshape(2, 1, D_MODEL)
    g_ffn = norm_ffn.reshape(2, 1, D_MODEL)
    g_final = norm_final.reshape(1, 1, D_MODEL)
    b_ffn = b_ffn_conv.reshape(2, 1, D_FF)

    pos_small = jnp.concatenate([jnp.arange(nb, dtype=jnp.int32),
                                 jnp.full((nb,), PAST_LEN, jnp.int32)])
    tabs_big = _rope_tables_split(N_META, SEQ)
    tabs_small = _rope_tables(pos_small)

    proj_s, w_ret_in_b, qkt_s = _ret_inproj_small(meta_tokens, x_samp, g_mix, w_ret_in, tabs_small)
    xs_meta, s_meta, w_ret_out_b = _ret_meta(proj_s, meta_tokens, w_ret_out)
    xb, ret_p, o_s, ret_s = _ret_layer(xb, g_mix, w_ret_in_b, tabs_big, s_meta, w_ret_out_b,
                                       proj_s, qkt_s, state_ret, chunk=256)
    xs_samp = _gn_outproj(o_s, proj_s, x_samp, w_ret_out_b)

    xs, ffn_s0, xb, ffn_p0 = _mlp_layer(
        (xs_meta, 0), (xs_samp, 0), xb, g_ffn, 0, w_ffn_in, w_ffn_conv, b_ffn, w_ffn_out,
        state_ffn, 0, g_final, kind="ffn", final_norm=False)

    zero_b = jnp.zeros((1, 1, D_MODEL), F32)
    xs, conv_s, xb, conv_p = _mlp_layer(
        (xs, 0), (xs, 1), xb, g_mix, 1, w_sc_in, w_sc_conv, zero_b, w_sc_out,
        state_conv, 0, g_final, kind="sc", final_norm=False)

    xs, ffn_s, xb, ffn_p = _mlp_layer(
        (xs, 0), (xs, 1), xb, g_ffn, 1, w_ffn_in, w_ffn_conv, b_ffn, w_ffn_out,
        state_ffn, 1, g_final, kind="ffn", final_norm=True, prev_states=(ffn_s0, ffn_p0))

    y_prompt = xb.reshape(BATCH, SEQ, D_MODEL)
    y_sample = xs[nb:].reshape(DEC_BATCH, 1, D_MODEL)
    return (y_prompt, y_sample, ret_p[None], ret_s, conv_p[None], conv_s[None], ffn_p, ffn_s)
```

```python
import functools
import math

import jax
import jax.numpy as jnp
from jax import lax
from jax.experimental import pallas as pl
from jax.experimental.pallas import tpu as pltpu

F32 = jnp.float32
BF16 = jnp.bfloat16

D_MODEL = 1024
SEQ = 2048
BATCH = 8
DEC_BATCH = 128
PAST_LEN = 16384
N_META = 16
RET_HEADS = 4
RET_DK = 256
RET_DV = 512
RET_VDIM = RET_HEADS * RET_DV
RET_IN = 2 * D_MODEL + 2 * RET_VDIM
ROPE_BASE = 10000.0
D_FF = 2816
NORM_EPS = 1e-6
GN_EPS = 1e-6

LANES = 128
SUBLANES = 8
SMALL_BLOCK = 128
SMALL_ROWS = 2 * SMALL_BLOCK
MLP_CW = 256
PROJ_CW = 1024
ROT_LO = 64
VMEM_LIMIT = 60 * 1024 * 1024


def _params(n_axes):
    return pltpu.CompilerParams(
        dimension_semantics=("arbitrary",) * n_axes,
        vmem_limit_bytes=VMEM_LIMIT)


def _const_spec(shape, index=None):
    index = (0,) * len(shape) if index is None else index
    return pl.BlockSpec(shape, lambda *_: index, pipeline_mode=pl.Buffered(1))


def _rms(x, g):
    ms = jnp.mean(x * x, axis=-1, keepdims=True)
    return x * lax.rsqrt(ms + NORM_EPS) * g


def _rotary(zs, col, c_ref, s_ref):
    sl = slice(col, col + LANES)
    even = lax.broadcasted_iota(jnp.int32, zs.shape, 1) % 2 == 0
    partner = jnp.where(even, pltpu.roll(zs, LANES - 1, 1), pltpu.roll(zs, 1, 1))
    return zs * c_ref[:, sl] + partner * s_ref[:, sl]


def _sample_state_update(p_ref, qkt_ref, s_ref, o_ref, sn_ref, first, count):
    tile = RET_DV // LANES
    for j in range(count):
        b = first + j
        row = p_ref[pl.ds(b, 1), :]
        lane_b = jnp.full((RET_DK, LANES), b, jnp.int32)
        for h in range(RET_HEADS):
            gamma = 1.0 - 2.0 ** (-5.0 - h)
            qs, ks, vs, _ = _head_cols(h)
            v = row[:, vs]
            q_rows = jnp.take_along_axis(qkt_ref[qs, :], lane_b, axis=1)
            k_rows = jnp.take_along_axis(qkt_ref[ks, :], lane_b, axis=1)
            s_prev = s_ref[j, h]
            cross = jnp.sum(s_prev * _lane_tile(q_rows, tile), axis=0, keepdims=True)
            qk = jnp.sum(row[:, qs] * row[:, ks], axis=-1, keepdims=True)
            o_ref[pl.ds(b, 1), h * RET_DV:(h + 1) * RET_DV] = qk * v + gamma * cross
            sn_ref[j, h] = gamma * s_prev + _lane_tile(k_rows, tile) * v


QK_CHUNKS = 2 * D_MODEL // PROJ_CW


def _ret_inproj_small_kernel(xm_ref, xs_ref, g_ref, w_ref, c_ref, s_ref,
                             o_ref, wb_ref, qkt_ref, h_scr):
    c = pl.program_id(0)

    @pl.when(c == 0)
    def _():
        g = g_ref[0:1, :]
        h_scr[0:N_META, :] = _rms(xm_ref[...], g).astype(BF16)
        h_scr[N_META:SMALL_BLOCK, :] = jnp.zeros((SMALL_BLOCK - N_META, D_MODEL), BF16)
        h_scr[SMALL_BLOCK:, :] = _rms(xs_ref[...], g).astype(BF16)

    wb = w_ref[...].astype(BF16)
    wb_ref[...] = wb
    z = jnp.dot(h_scr[...], wb, preferred_element_type=F32)

    @pl.when(c >= QK_CHUNKS)
    def _():
        o_ref[...] = z

    @pl.when(c < QK_CHUNKS)
    def _():
        scale = jnp.where(c >= QK_CHUNKS // 2, RET_DK ** -0.5, 1.0)
        for s0 in range(0, PROJ_CW, LANES):
            r = _rotary(z[:, s0:s0 + LANES], s0 % RET_DK, c_ref, s_ref) * scale
            o_ref[:, s0:s0 + LANES] = r
            qkt_ref[s0:s0 + LANES, :] = r[SMALL_BLOCK:].T


def _ret_inproj_small(x_meta, x_samp, g, w, tabs):
    return pl.pallas_call(
        _ret_inproj_small_kernel,
        grid=(RET_IN // PROJ_CW,),
        in_specs=[
            _const_spec((N_META, D_MODEL)),
            _const_spec((DEC_BATCH, None, D_MODEL), (0, 0, 0)),
            _const_spec(g.shape),
            pl.BlockSpec((None, D_MODEL, PROJ_CW), lambda c: (0, 0, c)),
            _const_spec((SMALL_ROWS, RET_DK)),
            _const_spec((SMALL_ROWS, RET_DK)),
        ],
        out_specs=[
            pl.BlockSpec((SMALL_ROWS, PROJ_CW), lambda c: (0, c)),
            pl.BlockSpec((D_MODEL, PROJ_CW), lambda c: (0, c)),
            pl.BlockSpec((PROJ_CW, DEC_BATCH), lambda c: (jnp.minimum(c, QK_CHUNKS - 1), 0)),
        ],
        out_shape=[
            jax.ShapeDtypeStruct((SMALL_ROWS, RET_IN), F32),
            jax.ShapeDtypeStruct((D_MODEL, RET_IN), BF16),
            jax.ShapeDtypeStruct((2 * D_MODEL, DEC_BATCH), F32),
        ],
        scratch_shapes=[pltpu.VMEM((SMALL_ROWS, D_MODEL), BF16)],
        compiler_params=_params(1),
        name="ret_inproj_small",
    )(x_meta, x_samp, g, w, *tabs)


def _gn_gate(o, g):
    mu = jnp.mean(o, axis=-1, keepdims=True)
    d = o - mu
    var = jnp.mean(d * d, axis=-1, keepdims=True)
    on = d * lax.rsqrt(var + GN_EPS)
    gf = g.astype(F32)
    return (gf / (1.0 + jnp.exp(-gf))) * on


def _lane_tile(x, n):
    return jnp.concatenate([x] * n, axis=1)


def _head_cols(h):
    q0 = h * RET_DK
    k0 = D_MODEL + h * RET_DK
    v0 = 2 * D_MODEL + h * RET_DV
    g0 = 2 * D_MODEL + RET_VDIM + h * RET_DV
    return (slice(q0, q0 + RET_DK), slice(k0, k0 + RET_DK),
            slice(v0, v0 + RET_DV), slice(g0, g0 + RET_DV))


def _decay_tables(decay_scr, cdec_scr, kdec_scr, chunk, valid):
    n_mat = lax.broadcasted_iota(jnp.int32, (chunk, chunk), 0)
    m_mat = lax.broadcasted_iota(jnp.int32, (chunk, chunk), 1)
    diff = (n_mat - m_mat).astype(F32)
    causal = diff >= 0
    diff = jnp.where(causal, diff, 0.0)
    n_col = lax.broadcasted_iota(jnp.int32, (chunk, LANES), 0).astype(F32)
    for h in range(RET_HEADS):
        lg = math.log(1.0 - 2.0 ** (-5.0 - h))
        decay_scr[h] = jnp.where(causal, jnp.exp(lg * diff), 0.0)
        cdec_scr[h] = jnp.exp(lg * (n_col + 1.0))
        kdec_scr[h] = jnp.where(n_col < valid, jnp.exp(lg * (valid - 1.0 - n_col)), 0.0)


def _ret_head(h, q, k, v, g, s_ref, decay_scr, cdec_scr, kdec_scr, gated_scr, valid):
    lg = math.log(1.0 - 2.0 ** (-5.0 - h))
    s_prev = s_ref[...]
    scores = lax.dot_general(q, k.astype(BF16), (((1,), (1,)), ((), ())),
                             preferred_element_type=F32)
    inner = jnp.dot((scores * decay_scr[h]).astype(BF16), v, preferred_element_type=F32)
    cross = jnp.dot(q, s_prev.astype(BF16), preferred_element_type=F32)
    o = inner + cross * _lane_tile(cdec_scr[h], RET_DV // LANES)
    kd = (k * _lane_tile(kdec_scr[h], RET_DK // LANES)).astype(BF16)
    kv = lax.dot_general(kd, v, (((0,), (0,)), ((), ())), preferred_element_type=F32)
    s_ref[...] = math.exp(lg * valid) * s_prev + kv
    gated_scr[:, h * RET_DV:(h + 1) * RET_DV] = _gn_gate(o, g).astype(BF16)


def _ret_meta_kernel(p_ref, x_ref, wo_ref, xo_ref, sf_ref, wob_ref,
                     gated_scr, decay_scr, cdec_scr, kdec_scr):
    _decay_tables(decay_scr, cdec_scr, kdec_scr, SMALL_BLOCK, N_META)
    sf_ref[...] = jnp.zeros_like(sf_ref)
    for h in range(RET_HEADS):
        qs, ks, vs, gs = _head_cols(h)
        _ret_head(h, p_ref[:, qs].astype(BF16), p_ref[:, ks], p_ref[:, vs].astype(BF16),
                  p_ref[:, gs], sf_ref.at[h], decay_scr, cdec_scr, kdec_scr, gated_scr, N_META)
    wo = wo_ref[...].astype(BF16)
    wob_ref[...] = wo
    out = jnp.dot(gated_scr[...], wo, preferred_element_type=F32)
    xo_ref[0:N_META, :] = x_ref[...] + out[0:N_META]
    xo_ref[N_META:, :] = out[N_META:]


def _ret_meta(proj_small, x_meta, w_out):
    nb = SMALL_BLOCK
    state_shape = (RET_HEADS, RET_DK, RET_DV)
    w_shape = (RET_VDIM, D_MODEL)
    return pl.pallas_call(
        _ret_meta_kernel,
        grid=(1,),
        in_specs=[
            _const_spec((nb, RET_IN)),
            _const_spec((N_META, D_MODEL)),
            _const_spec((None,) + w_shape, (0, 0, 0)),
        ],
        out_specs=[
            pl.BlockSpec((nb, D_MODEL), lambda i: (0, 0)),
            pl.BlockSpec(state_shape, lambda i: (0, 0, 0)),
            pl.BlockSpec(w_shape, lambda i: (0, 0)),
        ],
        out_shape=[
            jax.ShapeDtypeStruct((nb, D_MODEL), F32),
            jax.ShapeDtypeStruct(state_shape, F32),
            jax.ShapeDtypeStruct(w_shape, BF16),
        ],
        scratch_shapes=[
            pltpu.VMEM((nb, RET_VDIM), BF16),
            pltpu.VMEM((RET_HEADS, nb, nb), F32),
            pltpu.VMEM((RET_HEADS, nb, LANES), F32),
            pltpu.VMEM((RET_HEADS, nb, LANES), F32),
        ],
        compiler_params=_params(1),
        name="ret_meta",
    )(proj_small, x_meta, w_out)


def _ret_layer_kernel(x_ref, g_ref, w_ref, ta_ref, tb_ref, s0_ref, wo_ref, ps_ref,
                      qkt_ref, st_ref, xo_ref, sf_ref, os_ref, sn_ref,
                      gated_scr, decay_scr, cdec_scr, kdec_scr, c_ref, s_ref, *, chunk, bb):
    b = pl.program_id(0)
    c = pl.program_id(1)

    @pl.when((b == 0) & (c == 0))
    def _():
        _decay_tables(decay_scr, cdec_scr, kdec_scr, chunk, chunk)

    for blk in range(chunk // ROT_LO):
        rows = slice(blk * ROT_LO, (blk + 1) * ROT_LO)
        hi = pl.ds(c * (chunk // ROT_LO) + blk, 1)
        ca, sa = ta_ref[0, hi, :], ta_ref[1, hi, :]
        c_ref[rows, :] = ca * tb_ref[0] - sa * tb_ref[1]
        s_ref[rows, :] = sa * tb_ref[2] + ca * tb_ref[3]

    @pl.when(c == 0)
    def _():
        sf_ref[0] = s0_ref[...]

    step = b * pl.num_programs(1) + c
    _sample_state_update(ps_ref, qkt_ref, st_ref, os_ref, sn_ref, step * bb, bb)

    hn = _rms(x_ref[...], g_ref[0:1, :]).astype(BF16)
    proj = lambda sl: jnp.dot(hn, w_ref[:, sl], preferred_element_type=F32)
    rot = lambda z: jnp.concatenate(
        [_rotary(z[:, s0:s0 + LANES], s0, c_ref, s_ref)
         for s0 in range(0, RET_DK, LANES)], axis=1)
    for h in range(RET_HEADS):
        qs, ks, vs, gs = _head_cols(h)
        q = rot(proj(qs)).astype(BF16)
        k = rot(proj(ks)) * (RET_DK ** -0.5)
        _ret_head(h, q, k, proj(vs).astype(BF16), proj(gs), sf_ref.at[0, h],
                  decay_scr, cdec_scr, kdec_scr, gated_scr, chunk)
    xo_ref[...] = x_ref[...] + jnp.dot(gated_scr[...], wo_ref[...],
                                       preferred_element_type=F32)


def _ret_layer(x, g, w_in_b, tabs, s0, w_out_b, proj_small, qkt_small, state, *, chunk):
    n_chunks = SEQ // chunk
    steps = BATCH * n_chunks
    bb = DEC_BATCH // steps
    row_map = lambda b, c: (b * n_chunks + c, 0)
    state_shape = (RET_HEADS, RET_DK, RET_DV)
    samp_spec = pl.BlockSpec((None, bb) + state_shape,
                             lambda b, c: (0, b * n_chunks + c, 0, 0, 0))
    return pl.pallas_call(
        functools.partial(_ret_layer_kernel, chunk=chunk, bb=bb),
        grid=(BATCH, n_chunks),
        in_specs=[
            pl.BlockSpec((chunk, D_MODEL), row_map),
            _const_spec(g.shape),
            _const_spec((D_MODEL, RET_IN)),
            _const_spec((2, SEQ // ROT_LO, RET_DK)),
            _const_spec((4, ROT_LO, RET_DK)),
            _const_spec(state_shape),
            _const_spec((RET_VDIM, D_MODEL)),
            _const_spec((SMALL_BLOCK, RET_IN), (1, 0)),
            _const_spec((2 * D_MODEL, DEC_BATCH)),
            samp_spec,
        ],
        out_specs=[
            pl.BlockSpec((chunk, D_MODEL), row_map),
            pl.BlockSpec((1,) + state_shape, lambda b, c: (b, 0, 0, 0)),
            pl.BlockSpec((DEC_BATCH, RET_VDIM), lambda b, c: (0, 0)),
            samp_spec,
        ],
        out_shape=[
            jax.ShapeDtypeStruct(x.shape, F32),
            jax.ShapeDtypeStruct((BATCH,) + state_shape, F32),
            jax.ShapeDtypeStruct((DEC_BATCH, RET_VDIM), F32),
            jax.ShapeDtypeStruct(state.shape, F32),
        ],
        scratch_shapes=[
            pltpu.VMEM((chunk, RET_VDIM), BF16),
            pltpu.VMEM((RET_HEADS, chunk, chunk), F32),
            pltpu.VMEM((RET_HEADS, chunk, LANES), F32),
            pltpu.VMEM((RET_HEADS, chunk, LANES), F32),
            pltpu.VMEM((chunk, RET_DK), F32),
            pltpu.VMEM((chunk, RET_DK), F32),
        ],
        compiler_params=_params(2),
        name="ret_layer",
    )(x, g, w_in_b, *tabs, s0, w_out_b, proj_small, qkt_small, state)


def _gn_outproj_kernel(o_ref, g_ref, x_ref, wo_ref, xo_ref):
    parts = []
    for h in range(RET_HEADS):
        sl = slice(h * RET_DV, (h + 1) * RET_DV)
        parts.append(_gn_gate(o_ref[:, sl], g_ref[:, sl]).astype(BF16))
    gated = jnp.concatenate(parts, axis=1)
    xo_ref[...] = x_ref[...] + jnp.dot(gated, wo_ref[...], preferred_element_type=F32)


def _gn_outproj(o, proj_small, x_samp, w_out_b):
    return pl.pallas_call(
        _gn_outproj_kernel,
        grid=(1,),
        in_specs=[
            _const_spec((DEC_BATCH, RET_VDIM)),
            _const_spec((SMALL_BLOCK, RET_VDIM), (1, 2)),
            _const_spec((DEC_BATCH, None, D_MODEL), (0, 0, 0)),
            _const_spec((RET_VDIM, D_MODEL)),
        ],
        out_specs=pl.BlockSpec((DEC_BATCH, D_MODEL), lambda i: (0, 0)),
        out_shape=jax.ShapeDtypeStruct((DEC_BATCH, D_MODEL), F32),
        compiler_params=_params(1),
        name="gn_outproj",
    )(o, proj_small, x_samp, w_out_b)


def _n_parts(kind):
    return 2 if kind == "ffn" else 3


def _mlp_pre(parts, kind):
    if kind == "ffn":
        return parts[0], parts[1]
    return parts[1] * parts[2], parts[0]


def _mlp_act(pre, p1, p2, other, cw, cb, kind):
    conv = cw[0:1] * p2 + cw[1:2] * p1 + cw[2:3] * pre
    if kind == "ffn":
        a = conv + cb
        return (a / (1.0 + jnp.exp(-a))) * other
    return other * conv


def _mlp_layer_kernel(*refs, tm, dff, kind, final_norm, stacked, g_layer, layer):
    npart = _n_parts(kind)
    nb = SMALL_BLOCK
    nc = dff // MLP_CW
    tiles = SEQ // tm
    refs = list(refs)
    xm_ref, xsamp_ref, xb_ref, g_ref = refs[0:4]
    w_refs = refs[4:4 + npart]
    cw_ref, cb_ref, wo_ref, st_ref, gf_ref = refs[4 + npart:9 + npart]
    rest = refs[9 + npart:]
    if stacked:
        sn_prev_ref, stp_prev_ref = rest[0:2]
        rest = rest[2:]
    xso_ref, sn_ref, xbo_ref, stp_ref = rest[0:4]
    wb_scrs = rest[4:4 + npart]
    (wob_scr, h_scr, acc_scr, meta_carry_scr, carry_scr,
     pre_scr, act_scr) = rest[4 + npart:]
    sn_new = sn_ref.at[1] if stacked else sn_ref
    g_row = slice(g_layer, g_layer + 1)
    cb_row = slice(layer, layer + 1)
    i = pl.program_id(0)

    @pl.when(i < nc)
    def _small_step():
        col = pl.multiple_of(i * MLP_CW, MLP_CW)
        cols = pl.ds(col, MLP_CW)

        @pl.when(i == 0)
        def _():
            h_scr[0:nb, :] = _rms(xm_ref[...], g_ref[g_row, :]).astype(BF16)
            h_scr[nb:, :] = _rms(xsamp_ref[...], g_ref[g_row, :]).astype(BF16)
            acc_scr[...] = jnp.zeros_like(acc_scr)

        parts = []
        for w_ref, wb_scr in zip(w_refs, wb_scrs):
            wb = w_ref[...].astype(BF16)
            wb_scr[:, cols] = wb
            parts.append(jnp.dot(h_scr[...], wb, preferred_element_type=F32))
        pre, other = _mlp_pre(parts, kind)
        cw = cw_ref[:, cols]
        cb = cb_ref[cb_row, cols]
        pre_scr[0:SUBLANES, :] = jnp.zeros((SUBLANES, MLP_CW), F32)
        pre_scr[SUBLANES:SUBLANES + nb, :] = pre[0:nb]
        act_meta = _mlp_act(pre[0:nb], pre_scr[SUBLANES - 1:SUBLANES - 1 + nb, :],
                            pre_scr[SUBLANES - 2:SUBLANES - 2 + nb, :], other[0:nb],
                            cw, cb, kind)
        meta_carry_scr[:, cols] = pre[N_META - SUBLANES:N_META]
        s0 = st_ref[:, 0, :]
        s1 = st_ref[:, 1, :]
        sn_new[:, 0, :] = s1
        sn_new[:, 1, :] = pre[nb:]
        if stacked:
            sn_ref[0] = sn_prev_ref[...]
        act_samp = _mlp_act(pre[nb:], s1, s0, other[nb:], cw, cb, kind)
        act = jnp.concatenate([act_meta, act_samp], axis=0).astype(BF16)
        wob = wo_ref[...].astype(BF16)
        wob_scr[cols, :] = wob
        acc_scr[...] += jnp.dot(act, wob, preferred_element_type=F32)

        @pl.when(i == nc - 1)
        def _():
            for x_ref, rows in ((xm_ref, slice(0, nb)), (xsamp_ref, slice(nb, 2 * nb))):
                out = x_ref[...] + acc_scr[rows, :]
                if final_norm:
                    out = _rms(out, gf_ref[...])
                xso_ref[rows, :] = out

    @pl.when(i >= nc)
    def _big_step():
        @pl.when((i - nc) % tiles == 0)
        def _():
            carry_scr[...] = meta_carry_scr[...]

        h = _rms(xb_ref[...], g_ref[g_row, :]).astype(BF16)
        for c0 in range(0, dff, MLP_CW):
            sl = slice(c0, c0 + MLP_CW)
            parts = [jnp.dot(h, w[:, sl], preferred_element_type=F32) for w in wb_scrs]
            pre, other = _mlp_pre(parts, kind)
            head_rows = lax.broadcasted_iota(jnp.int32, (SUBLANES, MLP_CW), 0)
            prev = carry_scr[:, sl]
            shifted = []
            for d in (1, 2):
                rolled = pltpu.roll(pre, d, 0)
                first = jnp.where(head_rows < d, pltpu.roll(prev, d, 0), rolled[0:SUBLANES])
                shifted.append(jnp.concatenate([first, rolled[SUBLANES:]], axis=0))
            act_scr[:, sl] = _mlp_act(pre, shifted[0], shifted[1], other, cw_ref[:, sl],
                                      cb_ref[cb_row, sl], kind).astype(BF16)
            carry_scr[:, sl] = pre[tm - SUBLANES:tm, :]
        out = xb_ref[...] + jnp.dot(act_scr[...], wob_scr[...], preferred_element_type=F32)
        if final_norm:
            out = _rms(out, gf_ref[...])
        xbo_ref[...] = out
        if stacked:
            stp_ref[0] = stp_prev_ref[...]
            stp_ref[1, 0] = carry_scr[SUBLANES - 2:SUBLANES, :]
        else:
            stp_ref[0] = carry_scr[SUBLANES - 2:SUBLANES, :]


def _mlp_layer(x_meta, x_samp, xb, g, g_layer, w_in, conv_w, conv_b, w_out, state, layer, g_final,
               *, kind, final_norm, prev_states=None):
    npart = _n_parts(kind)
    dff = w_out.shape[1]
    nc = dff // MLP_CW
    stacked = prev_states is not None
    tm = 1024
    tiles = SEQ // tm
    chunk = lambda i: jnp.minimum(i, nc - 1)
    tile = lambda i: jnp.maximum(i - nc, 0)
    half_spec = lambda blk: _const_spec((SMALL_BLOCK, D_MODEL), (blk, 0))
    lead = (2,) if stacked else ()
    zero = (0,) if stacked else ()
    prev_specs = [
        pl.BlockSpec((DEC_BATCH, 2, MLP_CW), lambda i: (0, 0, chunk(i))),
        pl.BlockSpec((1, 2, dff), lambda i: (tile(i) // tiles, 0, 0)),
    ] if stacked else []
    return pl.pallas_call(
        functools.partial(_mlp_layer_kernel, tm=tm, dff=dff, kind=kind, final_norm=final_norm,
                          stacked=stacked, g_layer=g_layer, layer=layer),
        grid=(nc + BATCH * tiles,),
        in_specs=[
            half_spec(x_meta[1]),
            half_spec(x_samp[1]),
            pl.BlockSpec((tm, D_MODEL), lambda i: (tile(i), 0)),
            _const_spec(g.shape),
            *[pl.BlockSpec((None, D_MODEL, MLP_CW), functools.partial(
                lambda i, p: (layer, 0, p * nc + chunk(i)), p=p)) for p in range(npart)],
            _const_spec((None, 3, dff), (layer, 0, 0)),
            _const_spec(conv_b.shape),
            pl.BlockSpec((None, MLP_CW, D_MODEL), lambda i: (layer, chunk(i), 0)),
            pl.BlockSpec((None, DEC_BATCH, 2, MLP_CW), lambda i: (layer, 0, 0, chunk(i))),
            _const_spec((None, 1, D_MODEL), (0, 0, 0)),
            *prev_specs,
        ],
        out_specs=[
            pl.BlockSpec((SMALL_ROWS, D_MODEL), lambda i: (0, 0)),
            pl.BlockSpec(lead + (DEC_BATCH, 2, MLP_CW), lambda i: zero + (0, 0, chunk(i))),
            pl.BlockSpec((tm, D_MODEL), lambda i: (tile(i), 0)),
            pl.BlockSpec(lead + (1, 2, dff), lambda i: zero + (tile(i) // tiles, 0, 0)),
        ],
        out_shape=[
            jax.ShapeDtypeStruct((SMALL_ROWS, D_MODEL), F32),
            jax.ShapeDtypeStruct(lead + (DEC_BATCH, 2, dff), F32),
            jax.ShapeDtypeStruct(xb.shape, F32),
            jax.ShapeDtypeStruct(lead + (BATCH, 2, dff), F32),
        ],
        scratch_shapes=[
            *[pltpu.VMEM((D_MODEL, dff), BF16) for _ in range(npart)],
            pltpu.VMEM((dff, D_MODEL), BF16),
            pltpu.VMEM((SMALL_ROWS, D_MODEL), BF16),
            pltpu.VMEM((SMALL_ROWS, D_MODEL), F32),
            pltpu.VMEM((SUBLANES, dff), F32),
            pltpu.VMEM((SUBLANES, dff), F32),
            pltpu.VMEM((SMALL_BLOCK + SUBLANES, MLP_CW), F32),
            pltpu.VMEM((tm, dff), BF16),
        ],
        compiler_params=_params(1),
        name="mlp_layer_" + kind,
    )(x_meta[0], x_samp[0], xb, g, *([w_in] * npart), conv_w, conv_b, w_out, state, g_final,
      *(prev_states or ()))


def _rope_angles(pos):
    half = RET_DK // 2
    inv = 1.0 / (ROPE_BASE ** jnp.linspace(0.0, 1.0, half, dtype=F32))
    return pos.astype(F32)[:, None] * jnp.repeat(inv, 2)[None, :]


def _rope_sign():
    return jnp.where(jnp.arange(RET_DK) % 2 == 0, -1.0, 1.0).astype(F32)[None, :]


def _rope_tables(pos):
    ang = _rope_angles(pos)
    return jnp.cos(ang), jnp.sin(ang) * _rope_sign()


def _rope_tables_split(first, count):
    coarse = _rope_angles(first + ROT_LO * jnp.arange(count // ROT_LO, dtype=jnp.int32))
    fine = _rope_angles(jnp.arange(ROT_LO, dtype=jnp.int32))
    sign = _rope_sign()
    cf, sf = jnp.cos(fine), jnp.sin(fine)
    return (jnp.stack([jnp.cos(coarse), jnp.sin(coarse)]),
            jnp.stack([cf, sf, sign * cf, sign * sf]))


def kernel(x_prompt, x_sample, state_ret, state_conv, state_ffn, meta_tokens, norm_mix, norm_ffn,
           norm_final, w_ret_in, w_ret_out, w_sc_in, w_sc_conv, w_sc_out, w_ffn_in, w_ffn_conv,
           b_ffn_conv, w_ffn_out):
    nb = SMALL_BLOCK
    xb = x_prompt.reshape(BATCH * SEQ, D_MODEL)
    g_final = norm_final.reshape(1, 1, D_MODEL)

    pos_small = jnp.concatenate([jnp.arange(nb, dtype=jnp.int32),
                                 jnp.full((nb,), PAST_LEN, jnp.int32)])
    tabs_big = _rope_tables_split(N_META, SEQ)
    tabs_small = _rope_tables(pos_small)

    proj_s, w_ret_in_b, qkt_s = _ret_inproj_small(meta_tokens, x_sample, norm_mix, w_ret_in,
                                                  tabs_small)
    xs_meta, s_meta, w_ret_out_b = _ret_meta(proj_s, meta_tokens, w_ret_out)
    xb, ret_p, o_s, ret_s = _ret_layer(xb, norm_mix, w_ret_in_b, tabs_big, s_meta, w_ret_out_b,
                                       proj_s, qkt_s, state_ret, chunk=256)
    xs_samp = _gn_outproj(o_s, proj_s, x_sample, w_ret_out_b)

    xs, ffn_s0, xb, ffn_p0 = _mlp_layer(
        (xs_meta, 0), (xs_samp, 0), xb, norm_ffn, 0, w_ffn_in, w_ffn_conv, b_ffn_conv, w_ffn_out,
        state_ffn, 0, g_final, kind="ffn", final_norm=False)

    zero_b = jnp.zeros((1, D_MODEL), F32)
    xs, conv_s, xb, conv_p = _mlp_layer(
        (xs, 0), (xs, 1), xb, norm_mix, 1, w_sc_in, w_sc_conv, zero_b, w_sc_out,
        state_conv, 0, g_final, kind="sc", final_norm=False)

    xs, ffn_s, xb, ffn_p = _mlp_layer(
        (xs, 0), (xs, 1), xb, norm_ffn, 1, w_ffn_in, w_ffn_conv, b_ffn_conv, w_ffn_out,
        state_ffn, 1, g_final, kind="ffn", final_norm=True, prev_states=(ffn_s0, ffn_p0))

    y_prompt = xb.reshape(BATCH, SEQ, D_MODEL)
    y_sample = xs[nb:].reshape(DEC_BATCH, 1, D_MODEL)
    return (y_prompt, y_sample, ret_p[None], ret_s, conv_p[None], conv_s[None], ffn_p, ffn_s)
```

```python
import functools
import math

import jax
import jax.numpy as jnp
import numpy as np
from jax import lax
from jax.experimental import pallas as pl
from jax.experimental.pallas import tpu as pltpu

F32 = jnp.float32
BF16 = jnp.bfloat16

D_MODEL = 1024
SEQ = 2048
BATCH = 8
DEC_BATCH = 128
PAST_LEN = 16384
N_META = 16
RET_HEADS = 4
RET_DK = 256
RET_DV = 512
RET_VDIM = RET_HEADS * RET_DV
RET_IN = 2 * D_MODEL + 2 * RET_VDIM
ROPE_BASE = 10000.0
D_FF = 2816
NORM_EPS = 1e-6
GN_EPS = 1e-6

LANES = 128
SUBLANES = 8
SMALL_BLOCK = 128
SMALL_ROWS = 2 * SMALL_BLOCK
MLP_CW = 256
PROJ_CW = 1024
ROT_LO = 64
VMEM_LIMIT = 60 * 1024 * 1024


def _params(n_axes):
    return pltpu.CompilerParams(
        dimension_semantics=("arbitrary",) * n_axes,
        vmem_limit_bytes=VMEM_LIMIT)


def _const_spec(shape, index=None):
    index = (0,) * len(shape) if index is None else index
    return pl.BlockSpec(shape, lambda *_: index, pipeline_mode=pl.Buffered(1))


def _rms(x, g):
    ms = jnp.mean(x * x, axis=-1, keepdims=True)
    return x * lax.rsqrt(ms + NORM_EPS) * g


def _rotary(zs, col, c_ref, s_ref):
    sl = slice(col, col + LANES)
    even = lax.broadcasted_iota(jnp.int32, zs.shape, 1) % 2 == 0
    partner = jnp.where(even, pltpu.roll(zs, LANES - 1, 1), pltpu.roll(zs, 1, 1))
    return zs * c_ref[:, sl] + partner * s_ref[:, sl]


def _sample_state_update(p_ref, qkt_ref, s_ref, o_ref, sn_ref, first, count):
    tile = RET_DV // LANES
    for j in range(count):
        b = first + j
        row = p_ref[pl.ds(b, 1), :]
        lane_b = jnp.full((RET_DK, LANES), b, jnp.int32)
        for h in range(RET_HEADS):
            gamma = 1.0 - 2.0 ** (-5.0 - h)
            qs, ks, vs, _ = _head_cols(h)
            v = row[:, vs]
            q_rows = jnp.take_along_axis(qkt_ref[qs, :], lane_b, axis=1)
            k_rows = jnp.take_along_axis(qkt_ref[ks, :], lane_b, axis=1)
            s_prev = s_ref[j, h]
            cross = jnp.sum(s_prev * _lane_tile(q_rows, tile), axis=0, keepdims=True)
            qk = jnp.sum(row[:, qs] * row[:, ks], axis=-1, keepdims=True)
            o_ref[pl.ds(b, 1), h * RET_DV:(h + 1) * RET_DV] = qk * v + gamma * cross
            sn_ref[j, h] = gamma * s_prev + _lane_tile(k_rows, tile) * v


QK_CHUNKS = 2 * D_MODEL // PROJ_CW


def _ret_inproj_small_kernel(xm_ref, xs_ref, g_ref, w_ref, c_ref, s_ref,
                             o_ref, wb_ref, qkt_ref, h_scr):
    c = pl.program_id(0)

    @pl.when(c == 0)
    def _():
        g = g_ref[0:1, :]
        h_scr[0:N_META, :] = _rms(xm_ref[...], g).astype(BF16)
        h_scr[N_META:SMALL_BLOCK, :] = jnp.zeros((SMALL_BLOCK - N_META, D_MODEL), BF16)
        h_scr[SMALL_BLOCK:, :] = _rms(xs_ref[:, 0, :], g).astype(BF16)

    wb = w_ref[...].astype(BF16)
    wb_ref[...] = wb
    z = jnp.dot(h_scr[...], wb, preferred_element_type=F32)

    @pl.when(c >= QK_CHUNKS)
    def _():
        o_ref[...] = z

    @pl.when(c < QK_CHUNKS)
    def _():
        scale = jnp.where(c >= QK_CHUNKS // 2, RET_DK ** -0.5, 1.0)
        for s0 in range(0, PROJ_CW, LANES):
            r = _rotary(z[:, s0:s0 + LANES], s0 % RET_DK, c_ref, s_ref) * scale
            o_ref[:, s0:s0 + LANES] = r
            qkt_ref[s0:s0 + LANES, :] = r[SMALL_BLOCK:].T


def _ret_inproj_small(x_meta, x_samp, g, w, tabs):
    return pl.pallas_call(
        _ret_inproj_small_kernel,
        grid=(RET_IN // PROJ_CW,),
        in_specs=[
            _const_spec((N_META, D_MODEL)),
            _const_spec((DEC_BATCH, 1, D_MODEL)),
            _const_spec(g.shape),
            pl.BlockSpec((None, D_MODEL, PROJ_CW), lambda c: (0, 0, c)),
            _const_spec((SMALL_ROWS, RET_DK)),
            _const_spec((SMALL_ROWS, RET_DK)),
        ],
        out_specs=[
            pl.BlockSpec((SMALL_ROWS, PROJ_CW), lambda c: (0, c)),
            pl.BlockSpec((D_MODEL, PROJ_CW), lambda c: (0, c)),
            pl.BlockSpec((PROJ_CW, DEC_BATCH), lambda c: (jnp.minimum(c, QK_CHUNKS - 1), 0)),
        ],
        out_shape=[
            jax.ShapeDtypeStruct((SMALL_ROWS, RET_IN), F32),
            jax.ShapeDtypeStruct((D_MODEL, RET_IN), BF16),
            jax.ShapeDtypeStruct((2 * D_MODEL, DEC_BATCH), F32),
        ],
        scratch_shapes=[pltpu.VMEM((SMALL_ROWS, D_MODEL), BF16)],
        compiler_params=_params(1),
        name="ret_inproj_small",
    )(x_meta, x_samp, g, w, *tabs)


def _gn_gate(o, g):
    mu = jnp.mean(o, axis=-1, keepdims=True)
    d = o - mu
    var = jnp.mean(d * d, axis=-1, keepdims=True)
    on = d * lax.rsqrt(var + GN_EPS)
    gf = g.astype(F32)
    return (gf / (1.0 + jnp.exp(-gf))) * on


def _lane_tile(x, n):
    return jnp.concatenate([x] * n, axis=1)


def _head_cols(h):
    q0 = h * RET_DK
    k0 = D_MODEL + h * RET_DK
    v0 = 2 * D_MODEL + h * RET_DV
    g0 = 2 * D_MODEL + RET_VDIM + h * RET_DV
    return (slice(q0, q0 + RET_DK), slice(k0, k0 + RET_DK),
            slice(v0, v0 + RET_DV), slice(g0, g0 + RET_DV))


def _decay_tables(decay_scr, cdec_scr, kdec_scr, chunk, valid):
    n_mat = lax.broadcasted_iota(jnp.int32, (chunk, chunk), 0)
    m_mat = lax.broadcasted_iota(jnp.int32, (chunk, chunk), 1)
    diff = (n_mat - m_mat).astype(F32)
    causal = diff >= 0
    diff = jnp.where(causal, diff, 0.0)
    n_col = lax.broadcasted_iota(jnp.int32, (chunk, LANES), 0).astype(F32)
    for h in range(RET_HEADS):
        lg = math.log(1.0 - 2.0 ** (-5.0 - h))
        decay_scr[h] = jnp.where(causal, jnp.exp(lg * diff), 0.0)
        cdec_scr[h] = jnp.exp(lg * (n_col + 1.0))
        kdec_scr[h] = jnp.where(n_col < valid, jnp.exp(lg * (valid - 1.0 - n_col)), 0.0)


def _ret_head(h, q, k, v, g, s_ref, decay_scr, cdec_scr, kdec_scr, gated_scr, valid):
    lg = math.log(1.0 - 2.0 ** (-5.0 - h))
    s_prev = s_ref[...]
    scores = lax.dot_general(q, k.astype(BF16), (((1,), (1,)), ((), ())),
                             preferred_element_type=F32)
    inner = jnp.dot((scores * decay_scr[h]).astype(BF16), v, preferred_element_type=F32)
    cross = jnp.dot(q, s_prev.astype(BF16), preferred_element_type=F32)
    o = inner + cross * _lane_tile(cdec_scr[h], RET_DV // LANES)
    kd = (k * _lane_tile(kdec_scr[h], RET_DK // LANES)).astype(BF16)
    kv = lax.dot_general(kd, v, (((0,), (0,)), ((), ())), preferred_element_type=F32)
    s_ref[...] = math.exp(lg * valid) * s_prev + kv
    gated_scr[:, h * RET_DV:(h + 1) * RET_DV] = _gn_gate(o, g).astype(BF16)


def _ret_meta_kernel(p_ref, x_ref, wo_ref, xo_ref, sf_ref, wob_ref,
                     gated_scr, decay_scr, cdec_scr, kdec_scr):
    _decay_tables(decay_scr, cdec_scr, kdec_scr, SMALL_BLOCK, N_META)
    sf_ref[...] = jnp.zeros_like(sf_ref)
    for h in range(RET_HEADS):
        qs, ks, vs, gs = _head_cols(h)
        _ret_head(h, p_ref[:, qs].astype(BF16), p_ref[:, ks], p_ref[:, vs].astype(BF16),
                  p_ref[:, gs], sf_ref.at[h], decay_scr, cdec_scr, kdec_scr, gated_scr, N_META)
    wo = wo_ref[...].astype(BF16)
    wob_ref[...] = wo
    out = jnp.dot(gated_scr[...], wo, preferred_element_type=F32)
    xo_ref[0:N_META, :] = x_ref[...] + out[0:N_META]
    xo_ref[N_META:, :] = out[N_META:]


def _ret_meta(proj_small, x_meta, w_out):
    nb = SMALL_BLOCK
    state_shape = (RET_HEADS, RET_DK, RET_DV)
    w_shape = (RET_VDIM, D_MODEL)
    return pl.pallas_call(
        _ret_meta_kernel,
        grid=(1,),
        in_specs=[
            _const_spec((nb, RET_IN)),
            _const_spec((N_META, D_MODEL)),
            _const_spec((None,) + w_shape, (0, 0, 0)),
        ],
        out_specs=[
            pl.BlockSpec((nb, D_MODEL), lambda i: (0, 0)),
            pl.BlockSpec(state_shape, lambda i: (0, 0, 0)),
            pl.BlockSpec(w_shape, lambda i: (0, 0)),
        ],
        out_shape=[
            jax.ShapeDtypeStruct((nb, D_MODEL), F32),
            jax.ShapeDtypeStruct(state_shape, F32),
            jax.ShapeDtypeStruct(w_shape, BF16),
        ],
        scratch_shapes=[
            pltpu.VMEM((nb, RET_VDIM), BF16),
            pltpu.VMEM((RET_HEADS, nb, nb), F32),
            pltpu.VMEM((RET_HEADS, nb, LANES), F32),
            pltpu.VMEM((RET_HEADS, nb, LANES), F32),
        ],
        compiler_params=_params(1),
        name="ret_meta",
    )(proj_small, x_meta, w_out)


def _ret_layer_kernel(x_ref, g_ref, w_ref, ta_ref, tb_ref, s0_ref, wo_ref, ps_ref,
                      qkt_ref, st_ref, xo_ref, sf_ref, os_ref, sn_ref,
                      gated_scr, decay_scr, cdec_scr, kdec_scr, c_ref, s_ref, *, chunk, bb):
    b = pl.program_id(0)
    c = pl.program_id(1)

    @pl.when((b == 0) & (c == 0))
    def _():
        _decay_tables(decay_scr, cdec_scr, kdec_scr, chunk, chunk)

    for blk in range(chunk // ROT_LO):
        rows = slice(blk * ROT_LO, (blk + 1) * ROT_LO)
        hi = pl.ds(c * (chunk // ROT_LO) + blk, 1)
        ca, sa = ta_ref[0, hi, :], ta_ref[1, hi, :]
        c_ref[rows, :] = ca * tb_ref[0] - sa * tb_ref[1]
        s_ref[rows, :] = sa * tb_ref[2] + ca * tb_ref[3]

    @pl.when(c == 0)
    def _():
        sf_ref[0] = s0_ref[...]

    step = b * pl.num_programs(1) + c
    _sample_state_update(ps_ref, qkt_ref, st_ref, os_ref, sn_ref, step * bb, bb)

    hn = _rms(x_ref[...], g_ref[0:1, :]).astype(BF16)
    proj = lambda sl: jnp.dot(hn, w_ref[:, sl], preferred_element_type=F32)
    rot = lambda z: jnp.concatenate(
        [_rotary(z[:, s0:s0 + LANES], s0, c_ref, s_ref)
         for s0 in range(0, RET_DK, LANES)], axis=1)
    for h in range(RET_HEADS):
        qs, ks, vs, gs = _head_cols(h)
        q = rot(proj(qs)).astype(BF16)
        k = rot(proj(ks)) * (RET_DK ** -0.5)
        _ret_head(h, q, k, proj(vs).astype(BF16), proj(gs), sf_ref.at[0, h],
                  decay_scr, cdec_scr, kdec_scr, gated_scr, chunk)
    xo_ref[...] = x_ref[...] + jnp.dot(gated_scr[...], wo_ref[...],
                                       preferred_element_type=F32)


def _ret_layer(x, g, w_in_b, tabs, s0, w_out_b, proj_small, qkt_small, state, *, chunk):
    n_chunks = SEQ // chunk
    steps = BATCH * n_chunks
    bb = DEC_BATCH // steps
    row_map = lambda b, c: (b * n_chunks + c, 0)
    state_shape = (RET_HEADS, RET_DK, RET_DV)
    samp_spec = pl.BlockSpec((None, bb) + state_shape,
                             lambda b, c: (0, b * n_chunks + c, 0, 0, 0))
    return pl.pallas_call(
        functools.partial(_ret_layer_kernel, chunk=chunk, bb=bb),
        grid=(BATCH, n_chunks),
        in_specs=[
            pl.BlockSpec((chunk, D_MODEL), row_map),
            _const_spec(g.shape),
            _const_spec((D_MODEL, RET_IN)),
            _const_spec((2, SEQ // ROT_LO, RET_DK)),
            _const_spec((4, ROT_LO, RET_DK)),
            _const_spec(state_shape),
            _const_spec((RET_VDIM, D_MODEL)),
            _const_spec((SMALL_BLOCK, RET_IN), (1, 0)),
            _const_spec((2 * D_MODEL, DEC_BATCH)),
            samp_spec,
        ],
        out_specs=[
            pl.BlockSpec((chunk, D_MODEL), row_map),
            pl.BlockSpec((1,) + state_shape, lambda b, c: (b, 0, 0, 0)),
            pl.BlockSpec((DEC_BATCH, RET_VDIM), lambda b, c: (0, 0)),
            samp_spec,
        ],
        out_shape=[
            jax.ShapeDtypeStruct(x.shape, F32),
            jax.ShapeDtypeStruct((BATCH,) + state_shape, F32),
            jax.ShapeDtypeStruct((DEC_BATCH, RET_VDIM), F32),
            jax.ShapeDtypeStruct(state.shape, F32),
        ],
        scratch_shapes=[
            pltpu.VMEM((chunk, RET_VDIM), BF16),
            pltpu.VMEM((RET_HEADS, chunk, chunk), F32),
            pltpu.VMEM((RET_HEADS, chunk, LANES), F32),
            pltpu.VMEM((RET_HEADS, chunk, LANES), F32),
            pltpu.VMEM((chunk, RET_DK), F32),
            pltpu.VMEM((chunk, RET_DK), F32),
        ],
        compiler_params=_params(2),
        name="ret_layer",
    )(x, g, w_in_b, *tabs, s0, w_out_b, proj_small, qkt_small, state)


def _gn_outproj_kernel(o_ref, g_ref, x_ref, wo_ref, xo_ref):
    parts = []
    for h in range(RET_HEADS):
        sl = slice(h * RET_DV, (h + 1) * RET_DV)
        parts.append(_gn_gate(o_ref[:, sl], g_ref[:, sl]).astype(BF16))
    gated = jnp.concatenate(parts, axis=1)
    xo_ref[...] = x_ref[:, 0, :] + jnp.dot(gated, wo_ref[...], preferred_element_type=F32)


def _gn_outproj(o, proj_small, x_samp, w_out_b):
    return pl.pallas_call(
        _gn_outproj_kernel,
        grid=(1,),
        in_specs=[
            _const_spec((DEC_BATCH, RET_VDIM)),
            _const_spec((SMALL_BLOCK, RET_VDIM), (1, 2)),
            _const_spec((DEC_BATCH, 1, D_MODEL)),
            _const_spec((RET_VDIM, D_MODEL)),
        ],
        out_specs=pl.BlockSpec((DEC_BATCH, D_MODEL), lambda i: (0, 0)),
        out_shape=jax.ShapeDtypeStruct((DEC_BATCH, D_MODEL), F32),
        compiler_params=_params(1),
        name="gn_outproj",
    )(o, proj_small, x_samp, w_out_b)


def _n_parts(kind):
    return 2 if kind == "ffn" else 3


def _mlp_pre(parts, kind):
    if kind == "ffn":
        return parts[0], parts[1]
    return parts[1] * parts[2], parts[0]


def _mlp_act(pre, p1, p2, other, cw, cb, kind):
    conv = cw[0:1] * p2 + cw[1:2] * p1 + cw[2:3] * pre
    if kind == "ffn":
        a = conv + cb
        return (a / (1.0 + jnp.exp(-a))) * other
    return other * conv


def _mlp_layer_kernel(*refs, tm, dff, kind, final_norm, stacked, g_layer, layer):
    npart = _n_parts(kind)
    nb = SMALL_BLOCK
    nc = dff // MLP_CW
    tiles = SEQ // tm
    refs = list(refs)
    xm_ref, xsamp_ref, xb_ref, g_ref = refs[0:4]
    w_refs = refs[4:4 + npart]
    cw_ref, cb_ref, wo_ref, st_ref, gf_ref = refs[4 + npart:9 + npart]
    rest = refs[9 + npart:]
    if stacked:
        sn_prev_ref, stp_prev_ref = rest[0:2]
        rest = rest[2:]
    xso_ref, sn_ref, xbo_ref, stp_ref = rest[0:4]
    wb_scrs = rest[4:4 + npart]
    (wob_scr, h_scr, acc_scr, meta_carry_scr, carry_scr,
     pre_scr, act_scr) = rest[4 + npart:]
    sn_new = sn_ref.at[1] if stacked else sn_ref
    g_row = slice(g_layer, g_layer + 1)
    cb_row = slice(layer, layer + 1)
    i = pl.program_id(0)

    @pl.when(i < nc)
    def _small_step():
        col = pl.multiple_of(i * MLP_CW, MLP_CW)
        cols = pl.ds(col, MLP_CW)

        @pl.when(i == 0)
        def _():
            h_scr[0:nb, :] = _rms(xm_ref[...], g_ref[g_row, :]).astype(BF16)
            h_scr[nb:, :] = _rms(xsamp_ref[...], g_ref[g_row, :]).astype(BF16)
            acc_scr[...] = jnp.zeros_like(acc_scr)

        parts = []
        for w_ref, wb_scr in zip(w_refs, wb_scrs):
            wb = w_ref[...].astype(BF16)
            wb_scr[:, cols] = wb
            parts.append(jnp.dot(h_scr[...], wb, preferred_element_type=F32))
        pre, other = _mlp_pre(parts, kind)
        cw = cw_ref[layer, :, cols]
        cb = cb_ref[cb_row, cols]
        pre_scr[0:SUBLANES, :] = jnp.zeros((SUBLANES, MLP_CW), F32)
        pre_scr[SUBLANES:SUBLANES + nb, :] = pre[0:nb]
        act_meta = _mlp_act(pre[0:nb], pre_scr[SUBLANES - 1:SUBLANES - 1 + nb, :],
                            pre_scr[SUBLANES - 2:SUBLANES - 2 + nb, :], other[0:nb],
                            cw, cb, kind)
        meta_carry_scr[:, cols] = pre[N_META - SUBLANES:N_META]
        s0 = st_ref[:, 0, :]
        s1 = st_ref[:, 1, :]
        sn_new[:, 0, :] = s1
        sn_new[:, 1, :] = pre[nb:]
        if stacked:
            sn_ref[0] = sn_prev_ref[...]
        act_samp = _mlp_act(pre[nb:], s1, s0, other[nb:], cw, cb, kind)
        act = jnp.concatenate([act_meta, act_samp], axis=0).astype(BF16)
        wob = wo_ref[...].astype(BF16)
        wob_scr[cols, :] = wob
        acc_scr[...] += jnp.dot(act, wob, preferred_element_type=F32)

        @pl.when(i == nc - 1)
        def _():
            for x_ref, rows in ((xm_ref, slice(0, nb)), (xsamp_ref, slice(nb, 2 * nb))):
                out = x_ref[...] + acc_scr[rows, :]
                if final_norm:
                    out = _rms(out, gf_ref[...])
                xso_ref[rows, :] = out

    @pl.when(i >= nc)
    def _big_step():
        @pl.when((i - nc) % tiles == 0)
        def _():
            carry_scr[...] = meta_carry_scr[...]

        h = _rms(xb_ref[...], g_ref[g_row, :]).astype(BF16)
        for c0 in range(0, dff, MLP_CW):
            sl = slice(c0, c0 + MLP_CW)
            parts = [jnp.dot(h, w[:, sl], preferred_element_type=F32) for w in wb_scrs]
            pre, other = _mlp_pre(parts, kind)
            head_rows = lax.broadcasted_iota(jnp.int32, (SUBLANES, MLP_CW), 0)
            prev = carry_scr[:, sl]
            shifted = []
            for d in (1, 2):
                rolled = pltpu.roll(pre, d, 0)
                first = jnp.where(head_rows < d, pltpu.roll(prev, d, 0), rolled[0:SUBLANES])
                shifted.append(jnp.concatenate([first, rolled[SUBLANES:]], axis=0))
            act_scr[:, sl] = _mlp_act(pre, shifted[0], shifted[1], other, cw_ref[layer, :, sl],
                                      cb_ref[cb_row, sl], kind).astype(BF16)
            carry_scr[:, sl] = pre[tm - SUBLANES:tm, :]
        out = xb_ref[...] + jnp.dot(act_scr[...], wob_scr[...], preferred_element_type=F32)
        if final_norm:
            out = _rms(out, gf_ref[...])
        xbo_ref[...] = out
        if stacked:
            stp_ref[0] = stp_prev_ref[...]
            stp_ref[1, 0] = carry_scr[SUBLANES - 2:SUBLANES, :]
        else:
            stp_ref[0] = carry_scr[SUBLANES - 2:SUBLANES, :]


def _mlp_layer(x_meta, x_samp, xb, g, g_layer, w_in, conv_w, conv_b, w_out, state, layer, g_final,
               *, kind, final_norm, prev_states=None):
    npart = _n_parts(kind)
    dff = w_out.shape[1]
    nc = dff // MLP_CW
    stacked = prev_states is not None
    tm = 1024
    tiles = SEQ // tm
    chunk = lambda i: jnp.minimum(i, nc - 1)
    tile = lambda i: jnp.maximum(i - nc, 0)
    half_spec = lambda blk: _const_spec((SMALL_BLOCK, D_MODEL), (blk, 0))
    lead = (2,) if stacked else ()
    zero = (0,) if stacked else ()
    prev_specs = [
        pl.BlockSpec((DEC_BATCH, 2, MLP_CW), lambda i: (0, 0, chunk(i))),
        pl.BlockSpec((1, 2, dff), lambda i: (tile(i) // tiles, 0, 0)),
    ] if stacked else []
    return pl.pallas_call(
        functools.partial(_mlp_layer_kernel, tm=tm, dff=dff, kind=kind, final_norm=final_norm,
                          stacked=stacked, g_layer=g_layer, layer=layer),
        grid=(nc + BATCH * tiles,),
        in_specs=[
            half_spec(x_meta[1]),
            half_spec(x_samp[1]),
            pl.BlockSpec((tm, D_MODEL), lambda i: (tile(i), 0)),
            _const_spec(g.shape),
            *[pl.BlockSpec((None, D_MODEL, MLP_CW), functools.partial(
                lambda i, p: (layer, 0, p * nc + chunk(i)), p=p)) for p in range(npart)],
            _const_spec(conv_w.shape),
            _const_spec(conv_b.shape),
            pl.BlockSpec((None, MLP_CW, D_MODEL), lambda i: (layer, chunk(i), 0)),
            pl.BlockSpec((None, DEC_BATCH, 2, MLP_CW), lambda i: (layer, 0, 0, chunk(i))),
            _const_spec((None, 1, D_MODEL), (0, 0, 0)),
            *prev_specs,
        ],
        out_specs=[
            pl.BlockSpec((SMALL_ROWS, D_MODEL), lambda i: (0, 0)),
            pl.BlockSpec(lead + (DEC_BATCH, 2, MLP_CW), lambda i: zero + (0, 0, chunk(i))),
            pl.BlockSpec((tm, D_MODEL), lambda i: (tile(i), 0)),
            pl.BlockSpec(lead + (1, 2, dff), lambda i: zero + (tile(i) // tiles, 0, 0)),
        ],
        out_shape=[
            jax.ShapeDtypeStruct((SMALL_ROWS, D_MODEL), F32),
            jax.ShapeDtypeStruct(lead + (DEC_BATCH, 2, dff), F32),
            jax.ShapeDtypeStruct(xb.shape, F32),
            jax.ShapeDtypeStruct(lead + (BATCH, 2, dff), F32),
        ],
        scratch_shapes=[
            *[pltpu.VMEM((D_MODEL, dff), BF16) for _ in range(npart)],
            pltpu.VMEM((dff, D_MODEL), BF16),
            pltpu.VMEM((SMALL_ROWS, D_MODEL), BF16),
            pltpu.VMEM((SMALL_ROWS, D_MODEL), F32),
            pltpu.VMEM((SUBLANES, dff), F32),
            pltpu.VMEM((SUBLANES, dff), F32),
            pltpu.VMEM((SMALL_BLOCK + SUBLANES, MLP_CW), F32),
            pltpu.VMEM((tm, dff), BF16),
        ],
        compiler_params=_params(1),
        name="mlp_layer_" + kind,
    )(x_meta[0], x_samp[0], xb, g, *([w_in] * npart), conv_w, conv_b, w_out, state, g_final,
      *(prev_states or ()))


def _rope_angles(pos):
    half = RET_DK // 2
    inv = 1.0 / (ROPE_BASE ** jnp.linspace(0.0, 1.0, half, dtype=F32))
    return jnp.asarray(pos, F32)[:, None] * jnp.repeat(inv, 2)[None, :]


def _rope_sign():
    return jnp.where(jnp.arange(RET_DK) % 2 == 0, -1.0, 1.0).astype(F32)[None, :]


def _rope_tables(pos):
    ang = _rope_angles(pos)
    return jnp.cos(ang), jnp.sin(ang) * _rope_sign()


def _rope_tables_split(first, count):
    coarse = _rope_angles(first + ROT_LO * np.arange(count // ROT_LO))
    fine = _rope_angles(np.arange(ROT_LO))
    sign = _rope_sign()
    cf, sf = jnp.cos(fine), jnp.sin(fine)
    return (jnp.stack([jnp.cos(coarse), jnp.sin(coarse)]),
            jnp.stack([cf, sf, sign * cf, sign * sf]))


def kernel(x_prompt, x_sample, state_ret, state_conv, state_ffn, meta_tokens, norm_mix, norm_ffn,
           norm_final, w_ret_in, w_ret_out, w_sc_in, w_sc_conv, w_sc_out, w_ffn_in, w_ffn_conv,
           b_ffn_conv, w_ffn_out):
    nb = SMALL_BLOCK
    xb = x_prompt.reshape(BATCH * SEQ, D_MODEL)
    g_final = norm_final.reshape(1, 1, D_MODEL)

    pos_small = np.concatenate([np.arange(nb), np.full(nb, PAST_LEN)])
    tabs_big = _rope_tables_split(N_META, SEQ)
    tabs_small = _rope_tables(pos_small)

    proj_s, w_ret_in_b, qkt_s = _ret_inproj_small(meta_tokens, x_sample, norm_mix, w_ret_in,
                                                  tabs_small)
    xs_meta, s_meta, w_ret_out_b = _ret_meta(proj_s, meta_tokens, w_ret_out)
    xb, ret_p, o_s, ret_s = _ret_layer(xb, norm_mix, w_ret_in_b, tabs_big, s_meta, w_ret_out_b,
                                       proj_s, qkt_s, state_ret, chunk=256)
    xs_samp = _gn_outproj(o_s, proj_s, x_sample, w_ret_out_b)

    xs, ffn_s0, xb, ffn_p0 = _mlp_layer(
        (xs_meta, 0), (xs_samp, 0), xb, norm_ffn, 0, w_ffn_in, w_ffn_conv, b_ffn_conv, w_ffn_out,
        state_ffn, 0, g_final, kind="ffn", final_norm=False)

    zero_b = np.zeros((1, D_MODEL), np.float32)
    xs, conv_s, xb, conv_p = _mlp_layer(
        (xs, 0), (xs, 1), xb, norm_mix, 1, w_sc_in, w_sc_conv, zero_b, w_sc_out,
        state_conv, 0, g_final, kind="sc", final_norm=False)

    xs, ffn_s, xb, ffn_p = _mlp_layer(
        (xs, 0), (xs, 1), xb, norm_ffn, 1, w_ffn_in, w_ffn_conv, b_ffn_conv, w_ffn_out,
        state_ffn, 1, g_final, kind="ffn", final_norm=True, prev_states=(ffn_s0, ffn_p0))

    y_prompt = xb.reshape(BATCH, SEQ, D_MODEL)
    y_sample = xs[nb:].reshape(DEC_BATCH, 1, D_MODEL)
    return (y_prompt, y_sample, ret_p[None], ret_s, conv_p[None], conv_s[None], ffn_p, ffn_s)
```

```python
import functools
import math

import jax
import jax.numpy as jnp
import numpy as np
from jax import lax
from jax.experimental import pallas as pl
from jax.experimental.pallas import tpu as pltpu

F32 = jnp.float32
BF16 = jnp.bfloat16

D_MODEL = 1024
SEQ = 2048
BATCH = 8
DEC_BATCH = 128
PAST_LEN = 16384
N_META = 16
RET_HEADS = 4
RET_DK = 256
RET_DV = 512
RET_VDIM = RET_HEADS * RET_DV
RET_IN = 2 * D_MODEL + 2 * RET_VDIM
ROPE_BASE = 10000.0
D_FF = 2816
NORM_EPS = 1e-6
GN_EPS = 1e-6

LANES = 128
SUBLANES = 8
SMALL_BLOCK = 128
SMALL_ROWS = 2 * SMALL_BLOCK
MLP_CW = 256
PROJ_CW = 1024
ROT_LO = 64
VMEM_LIMIT = 60 * 1024 * 1024


def _params(n_axes):
    return pltpu.CompilerParams(
        dimension_semantics=("arbitrary",) * n_axes,
        vmem_limit_bytes=VMEM_LIMIT)


def _const_spec(shape, index=None):
    index = (0,) * len(shape) if index is None else index
    return pl.BlockSpec(shape, lambda *_: index, pipeline_mode=pl.Buffered(1))


def _rms(x, g):
    ms = jnp.mean(x * x, axis=-1, keepdims=True)
    return x * lax.rsqrt(ms + NORM_EPS) * g


def _rotary(zs, col, c_ref, s_ref):
    sl = slice(col, col + LANES)
    even = lax.broadcasted_iota(jnp.int32, zs.shape, 1) % 2 == 0
    partner = jnp.where(even, pltpu.roll(zs, LANES - 1, 1), pltpu.roll(zs, 1, 1))
    return zs * c_ref[:, sl] + partner * s_ref[:, sl]


def _sample_state_update(p_ref, qkt_ref, s_ref, o_ref, sn_ref, first, count):
    tile = RET_DV // LANES
    for j in range(count):
        b = first + j
        row = p_ref[pl.ds(b, 1), :]
        lane_b = jnp.full((RET_DK, LANES), b, jnp.int32)
        for h in range(RET_HEADS):
            gamma = 1.0 - 2.0 ** (-5.0 - h)
            qs, ks, vs, _ = _head_cols(h)
            v = row[:, vs]
            q_rows = jnp.take_along_axis(qkt_ref[qs, :], lane_b, axis=1)
            k_rows = jnp.take_along_axis(qkt_ref[ks, :], lane_b, axis=1)
            s_prev = s_ref[j, h]
            cross = jnp.sum(s_prev * _lane_tile(q_rows, tile), axis=0, keepdims=True)
            qk = jnp.sum(row[:, qs] * row[:, ks], axis=-1, keepdims=True)
            o_ref[pl.ds(b, 1), h * RET_DV:(h + 1) * RET_DV] = qk * v + gamma * cross
            sn_ref[j, h] = gamma * s_prev + _lane_tile(k_rows, tile) * v


QK_CHUNKS = 2 * D_MODEL // PROJ_CW


def _ret_inproj_small_kernel(xm_ref, xs_ref, g_ref, w_ref, c_ref, s_ref,
                             o_ref, wb_ref, qkt_ref, h_scr):
    c = pl.program_id(0)

    @pl.when(c == 0)
    def _():
        g = g_ref[0:1, :]
        h_scr[0:N_META, :] = _rms(xm_ref[...], g).astype(BF16)
        h_scr[N_META:SMALL_BLOCK, :] = jnp.zeros((SMALL_BLOCK - N_META, D_MODEL), BF16)
        h_scr[SMALL_BLOCK:, :] = _rms(xs_ref[:, 0, :], g).astype(BF16)

    wb = w_ref[...].astype(BF16)
    wb_ref[...] = wb
    z = jnp.dot(h_scr[...], wb, preferred_element_type=F32)

    @pl.when(c >= QK_CHUNKS)
    def _():
        o_ref[...] = z

    @pl.when(c < QK_CHUNKS)
    def _():
        scale = jnp.where(c >= QK_CHUNKS // 2, RET_DK ** -0.5, 1.0)
        for s0 in range(0, PROJ_CW, LANES):
            r = _rotary(z[:, s0:s0 + LANES], s0 % RET_DK, c_ref, s_ref) * scale
            o_ref[:, s0:s0 + LANES] = r
            qkt_ref[s0:s0 + LANES, :] = r[SMALL_BLOCK:].T


def _ret_inproj_small(x_meta, x_samp, g, w, tabs):
    return pl.pallas_call(
        _ret_inproj_small_kernel,
        grid=(RET_IN // PROJ_CW,),
        in_specs=[
            _const_spec((N_META, D_MODEL)),
            _const_spec((DEC_BATCH, 1, D_MODEL)),
            _const_spec(g.shape),
            pl.BlockSpec((None, D_MODEL, PROJ_CW), lambda c: (0, 0, c)),
            _const_spec((SMALL_ROWS, RET_DK)),
            _const_spec((SMALL_ROWS, RET_DK)),
        ],
        out_specs=[
            pl.BlockSpec((SMALL_ROWS, PROJ_CW), lambda c: (0, c)),
            pl.BlockSpec((D_MODEL, PROJ_CW), lambda c: (0, c)),
            pl.BlockSpec((PROJ_CW, DEC_BATCH), lambda c: (jnp.minimum(c, QK_CHUNKS - 1), 0)),
        ],
        out_shape=[
            jax.ShapeDtypeStruct((SMALL_ROWS, RET_IN), F32),
            jax.ShapeDtypeStruct((D_MODEL, RET_IN), BF16),
            jax.ShapeDtypeStruct((2 * D_MODEL, DEC_BATCH), F32),
        ],
        scratch_shapes=[pltpu.VMEM((SMALL_ROWS, D_MODEL), BF16)],
        compiler_params=_params(1),
        name="ret_inproj_small",
    )(x_meta, x_samp, g, w, *tabs)


def _gn_gate(o, g):
    mu = jnp.mean(o, axis=-1, keepdims=True)
    d = o - mu
    var = jnp.mean(d * d, axis=-1, keepdims=True)
    on = d * lax.rsqrt(var + GN_EPS)
    gb = g.astype(BF16)
    return (gb / (1.0 + jnp.exp(-gb))) * on.astype(BF16)


def _lane_tile(x, n):
    return jnp.concatenate([x] * n, axis=1)


def _head_cols(h):
    q0 = h * RET_DK
    k0 = D_MODEL + h * RET_DK
    v0 = 2 * D_MODEL + h * RET_DV
    g0 = 2 * D_MODEL + RET_VDIM + h * RET_DV
    return (slice(q0, q0 + RET_DK), slice(k0, k0 + RET_DK),
            slice(v0, v0 + RET_DV), slice(g0, g0 + RET_DV))


def _decay_tables(decay_scr, cdec_scr, kdec_scr, chunk, valid):
    n_mat = lax.broadcasted_iota(jnp.int32, (chunk, chunk), 0)
    m_mat = lax.broadcasted_iota(jnp.int32, (chunk, chunk), 1)
    diff = (n_mat - m_mat).astype(F32)
    causal = diff >= 0
    diff = jnp.where(causal, diff, 0.0)
    n_col = lax.broadcasted_iota(jnp.int32, (chunk, LANES), 0).astype(F32)
    for h in range(RET_HEADS):
        lg = math.log(1.0 - 2.0 ** (-5.0 - h))
        decay_scr[h] = jnp.where(causal, jnp.exp(lg * diff), 0.0)
        cdec_scr[h] = jnp.exp(lg * (n_col + 1.0))
        kdec_scr[h] = jnp.where(n_col < valid, jnp.exp(lg * (valid - 1.0 - n_col)), 0.0)


def _ret_head(h, q, k, v, g, s_ref, decay_scr, cdec_scr, kdec_scr, gated_scr, valid):
    lg = math.log(1.0 - 2.0 ** (-5.0 - h))
    s_prev = s_ref[...]
    scores = lax.dot_general(q, k.astype(BF16), (((1,), (1,)), ((), ())),
                             preferred_element_type=F32)
    inner = jnp.dot((scores * decay_scr[h]).astype(BF16), v, preferred_element_type=F32)
    cross = jnp.dot(q, s_prev.astype(BF16), preferred_element_type=F32)
    o = inner + cross * _lane_tile(cdec_scr[h], RET_DV // LANES)
    kd = (k * _lane_tile(kdec_scr[h], RET_DK // LANES)).astype(BF16)
    kv = lax.dot_general(kd, v, (((0,), (0,)), ((), ())), preferred_element_type=F32)
    s_ref[...] = math.exp(lg * valid) * s_prev + kv
    gated_scr[:, h * RET_DV:(h + 1) * RET_DV] = _gn_gate(o, g).astype(BF16)


def _ret_meta_kernel(p_ref, x_ref, wo_ref, xo_ref, sf_ref, wob_ref,
                     gated_scr, decay_scr, cdec_scr, kdec_scr):
    _decay_tables(decay_scr, cdec_scr, kdec_scr, SMALL_BLOCK, N_META)
    sf_ref[...] = jnp.zeros_like(sf_ref)
    for h in range(RET_HEADS):
        qs, ks, vs, gs = _head_cols(h)
        _ret_head(h, p_ref[:, qs].astype(BF16), p_ref[:, ks], p_ref[:, vs].astype(BF16),
                  p_ref[:, gs], sf_ref.at[h], decay_scr, cdec_scr, kdec_scr, gated_scr, N_META)
    wo = wo_ref[...].astype(BF16)
    wob_ref[...] = wo
    out = jnp.dot(gated_scr[...], wo, preferred_element_type=F32)
    xo_ref[0:N_META, :] = x_ref[...] + out[0:N_META]
    xo_ref[N_META:, :] = out[N_META:]


def _ret_meta(proj_small, x_meta, w_out):
    nb = SMALL_BLOCK
    state_shape = (RET_HEADS, RET_DK, RET_DV)
    w_shape = (RET_VDIM, D_MODEL)
    return pl.pallas_call(
        _ret_meta_kernel,
        grid=(1,),
        in_specs=[
            _const_spec((nb, RET_IN)),
            _const_spec((N_META, D_MODEL)),
            _const_spec((None,) + w_shape, (0, 0, 0)),
        ],
        out_specs=[
            pl.BlockSpec((nb, D_MODEL), lambda i: (0, 0)),
            pl.BlockSpec(state_shape, lambda i: (0, 0, 0)),
            pl.BlockSpec(w_shape, lambda i: (0, 0)),
        ],
        out_shape=[
            jax.ShapeDtypeStruct((nb, D_MODEL), F32),
            jax.ShapeDtypeStruct(state_shape, F32),
            jax.ShapeDtypeStruct(w_shape, BF16),
        ],
        scratch_shapes=[
            pltpu.VMEM((nb, RET_VDIM), BF16),
            pltpu.VMEM((RET_HEADS, nb, nb), F32),
            pltpu.VMEM((RET_HEADS, nb, LANES), F32),
            pltpu.VMEM((RET_HEADS, nb, LANES), F32),
        ],
        compiler_params=_params(1),
        name="ret_meta",
    )(proj_small, x_meta, w_out)


def _ret_layer_kernel(x_ref, g_ref, w_ref, ta_ref, tb_ref, s0_ref, wo_ref, ps_ref,
                      qkt_ref, st_ref, xo_ref, sf_ref, os_ref, sn_ref,
                      gated_scr, decay_scr, cdec_scr, kdec_scr, c_ref, s_ref, *, chunk, bb):
    b = pl.program_id(0)
    c = pl.program_id(1)

    @pl.when((b == 0) & (c == 0))
    def _():
        _decay_tables(decay_scr, cdec_scr, kdec_scr, chunk, chunk)

    for blk in range(chunk // ROT_LO):
        rows = slice(blk * ROT_LO, (blk + 1) * ROT_LO)
        hi = pl.ds(c * (chunk // ROT_LO) + blk, 1)
        ca, sa = ta_ref[0, hi, :], ta_ref[1, hi, :]
        c_ref[rows, :] = ca * tb_ref[0] - sa * tb_ref[1]
        s_ref[rows, :] = sa * tb_ref[2] + ca * tb_ref[3]

    @pl.when(c == 0)
    def _():
        sf_ref[0] = s0_ref[...]

    step = b * pl.num_programs(1) + c
    _sample_state_update(ps_ref, qkt_ref, st_ref, os_ref, sn_ref, step * bb, bb)

    hn = _rms(x_ref[...], g_ref[0:1, :]).astype(BF16)
    proj = lambda sl: jnp.dot(hn, w_ref[:, sl], preferred_element_type=F32)
    rot = lambda z: jnp.concatenate(
        [_rotary(z[:, s0:s0 + LANES], s0, c_ref, s_ref)
         for s0 in range(0, RET_DK, LANES)], axis=1)
    for h in range(RET_HEADS):
        qs, ks, vs, gs = _head_cols(h)
        q = rot(proj(qs)).astype(BF16)
        k = rot(proj(ks)) * (RET_DK ** -0.5)
        _ret_head(h, q, k, proj(vs).astype(BF16), proj(gs), sf_ref.at[0, h],
                  decay_scr, cdec_scr, kdec_scr, gated_scr, chunk)
    xo_ref[...] = x_ref[...] + jnp.dot(gated_scr[...], wo_ref[...],
                                       preferred_element_type=F32)


def _ret_layer(x, g, w_in_b, tabs, s0, w_out_b, proj_small, qkt_small, state, *, chunk):
    n_chunks = SEQ // chunk
    steps = BATCH * n_chunks
    bb = DEC_BATCH // steps
    row_map = lambda b, c: (b * n_chunks + c, 0)
    state_shape = (RET_HEADS, RET_DK, RET_DV)
    samp_spec = pl.BlockSpec((None, bb) + state_shape,
                             lambda b, c: (0, b * n_chunks + c, 0, 0, 0))
    return pl.pallas_call(
        functools.partial(_ret_layer_kernel, chunk=chunk, bb=bb),
        grid=(BATCH, n_chunks),
        in_specs=[
            pl.BlockSpec((chunk, D_MODEL), row_map),
            _const_spec(g.shape),
            _const_spec((D_MODEL, RET_IN)),
            _const_spec((2, SEQ // ROT_LO, RET_DK)),
            _const_spec((4, ROT_LO, RET_DK)),
            _const_spec(state_shape),
            _const_spec((RET_VDIM, D_MODEL)),
            _const_spec((SMALL_BLOCK, RET_IN), (1, 0)),
            _const_spec((2 * D_MODEL, DEC_BATCH)),
            samp_spec,
        ],
        out_specs=[
            pl.BlockSpec((chunk, D_MODEL), row_map),
            pl.BlockSpec((1,) + state_shape, lambda b, c: (b, 0, 0, 0)),
            pl.BlockSpec((DEC_BATCH, RET_VDIM), lambda b, c: (0, 0)),
            samp_spec,
        ],
        out_shape=[
            jax.ShapeDtypeStruct(x.shape, F32),
            jax.ShapeDtypeStruct((BATCH,) + state_shape, F32),
            jax.ShapeDtypeStruct((DEC_BATCH, RET_VDIM), F32),
            jax.ShapeDtypeStruct(state.shape, F32),
        ],
        scratch_shapes=[
            pltpu.VMEM((chunk, RET_VDIM), BF16),
            pltpu.VMEM((RET_HEADS, chunk, chunk), F32),
            pltpu.VMEM((RET_HEADS, chunk, LANES), F32),
            pltpu.VMEM((RET_HEADS, chunk, LANES), F32),
            pltpu.VMEM((chunk, RET_DK), F32),
            pltpu.VMEM((chunk, RET_DK), F32),
        ],
        compiler_params=_params(2),
        name="ret_layer",
    )(x, g, w_in_b, *tabs, s0, w_out_b, proj_small, qkt_small, state)


def _gn_outproj_kernel(o_ref, g_ref, x_ref, wo_ref, xo_ref):
    parts = []
    for h in range(RET_HEADS):
        sl = slice(h * RET_DV, (h + 1) * RET_DV)
        parts.append(_gn_gate(o_ref[:, sl], g_ref[:, sl]).astype(BF16))
    gated = jnp.concatenate(parts, axis=1)
    xo_ref[...] = x_ref[:, 0, :] + jnp.dot(gated, wo_ref[...], preferred_element_type=F32)


def _gn_outproj(o, proj_small, x_samp, w_out_b):
    return pl.pallas_call(
        _gn_outproj_kernel,
        grid=(1,),
        in_specs=[
            _const_spec((DEC_BATCH, RET_VDIM)),
            _const_spec((SMALL_BLOCK, RET_VDIM), (1, 2)),
            _const_spec((DEC_BATCH, 1, D_MODEL)),
            _const_spec((RET_VDIM, D_MODEL)),
        ],
        out_specs=pl.BlockSpec((DEC_BATCH, D_MODEL), lambda i: (0, 0)),
        out_shape=jax.ShapeDtypeStruct((DEC_BATCH, D_MODEL), F32),
        compiler_params=_params(1),
        name="gn_outproj",
    )(o, proj_small, x_samp, w_out_b)


def _n_parts(kind):
    return 2 if kind == "ffn" else 3


def _mlp_pre(parts, kind):
    if kind == "ffn":
        return parts[0], parts[1]
    return parts[1] * parts[2], parts[0]


def _mlp_act(pre, p1, p2, other, cw, cb, kind):
    conv = cw[0:1] * p2 + cw[1:2] * p1 + cw[2:3] * pre
    if kind == "ffn":
        a = conv + cb
        return (a / (1.0 + jnp.exp(-a))) * other
    return other * conv


def _mlp_layer_kernel(*refs, tm, dff, kind, final_norm, stacked, g_layer, layer):
    npart = _n_parts(kind)
    nb = SMALL_BLOCK
    nc = dff // MLP_CW
    tiles = SEQ // tm
    refs = list(refs)
    xm_ref, xsamp_ref, xb_ref, g_ref = refs[0:4]
    w_refs = refs[4:4 + npart]
    cw_ref, cb_ref, wo_ref, st_ref, gf_ref = refs[4 + npart:9 + npart]
    rest = refs[9 + npart:]
    if stacked:
        sn_prev_ref, stp_prev_ref = rest[0:2]
        rest = rest[2:]
    xso_ref, sn_ref, xbo_ref, stp_ref = rest[0:4]
    wb_scrs = rest[4:4 + npart]
    (wob_scr, h_scr, acc_scr, meta_carry_scr, carry_scr,
     pre_scr, act_scr) = rest[4 + npart:]
    sn_new = sn_ref.at[1] if stacked else sn_ref
    g_row = slice(g_layer, g_layer + 1)
    cb_row = slice(layer, layer + 1)
    i = pl.program_id(0)

    @pl.when(i < nc)
    def _small_step():
        col = pl.multiple_of(i * MLP_CW, MLP_CW)
        cols = pl.ds(col, MLP_CW)

        @pl.when(i == 0)
        def _():
            h_scr[0:nb, :] = _rms(xm_ref[...], g_ref[g_row, :]).astype(BF16)
            h_scr[nb:, :] = _rms(xsamp_ref[...], g_ref[g_row, :]).astype(BF16)
            acc_scr[...] = jnp.zeros_like(acc_scr)

        parts = []
        for w_ref, wb_scr in zip(w_refs, wb_scrs):
            wb = w_ref[...].astype(BF16)
            wb_scr[:, cols] = wb
            parts.append(jnp.dot(h_scr[...], wb, preferred_element_type=F32))
        pre, other = _mlp_pre(parts, kind)
        cw = cw_ref[layer, :, cols]
        cb = cb_ref[cb_row, cols]
        pre_scr[0:SUBLANES, :] = jnp.zeros((SUBLANES, MLP_CW), F32)
        pre_scr[SUBLANES:SUBLANES + nb, :] = pre[0:nb]
        act_meta = _mlp_act(pre[0:nb], pre_scr[SUBLANES - 1:SUBLANES - 1 + nb, :],
                            pre_scr[SUBLANES - 2:SUBLANES - 2 + nb, :], other[0:nb],
                            cw, cb, kind)
        meta_carry_scr[:, cols] = pre[N_META - SUBLANES:N_META]
        s0 = st_ref[:, 0, :]
        s1 = st_ref[:, 1, :]
        sn_new[:, 0, :] = s1
        sn_new[:, 1, :] = pre[nb:]
        if stacked:
            sn_ref[0] = sn_prev_ref[...]
        act_samp = _mlp_act(pre[nb:], s1, s0, other[nb:], cw, cb, kind)
        act = jnp.concatenate([act_meta, act_samp], axis=0).astype(BF16)
        wob = wo_ref[...].astype(BF16)
        wob_scr[cols, :] = wob
        acc_scr[...] += jnp.dot(act, wob, preferred_element_type=F32)

        @pl.when(i == nc - 1)
        def _():
            for x_ref, rows in ((xm_ref, slice(0, nb)), (xsamp_ref, slice(nb, 2 * nb))):
                out = x_ref[...] + acc_scr[rows, :]
                if final_norm:
                    out = _rms(out, gf_ref[...])
                xso_ref[rows, :] = out

    @pl.when(i >= nc)
    def _big_step():
        @pl.when((i - nc) % tiles == 0)
        def _():
            carry_scr[...] = meta_carry_scr[...]

        h = _rms(xb_ref[...], g_ref[g_row, :]).astype(BF16)
        for c0 in range(0, dff, MLP_CW):
            sl = slice(c0, c0 + MLP_CW)
            parts = [jnp.dot(h, w[:, sl], preferred_element_type=F32) for w in wb_scrs]
            pre, other = _mlp_pre(parts, kind)
            head_rows = lax.broadcasted_iota(jnp.int32, (SUBLANES, MLP_CW), 0)
            prev = carry_scr[:, sl]
            shifted = []
            for d in (1, 2):
                rolled = pltpu.roll(pre, d, 0)
                first = jnp.where(head_rows < d, pltpu.roll(prev, d, 0), rolled[0:SUBLANES])
                shifted.append(jnp.concatenate([first, rolled[SUBLANES:]], axis=0))
            act_scr[:, sl] = _mlp_act(pre, shifted[0], shifted[1], other, cw_ref[layer, :, sl],
                                      cb_ref[cb_row, sl], kind).astype(BF16)
            carry_scr[:, sl] = pre[tm - SUBLANES:tm, :]
        out = xb_ref[...] + jnp.dot(act_scr[...], wob_scr[...], preferred_element_type=F32)
        if final_norm:
            out = _rms(out, gf_ref[...])
        xbo_ref[...] = out
        if stacked:
            stp_ref[0] = stp_prev_ref[...]
            stp_ref[1, 0] = carry_scr[SUBLANES - 2:SUBLANES, :]
        else:
            stp_ref[0] = carry_scr[SUBLANES - 2:SUBLANES, :]


def _mlp_layer(x_meta, x_samp, xb, g, g_layer, w_in, conv_w, conv_b, w_out, state, layer, g_final,
               *, kind, final_norm, prev_states=None):
    npart = _n_parts(kind)
    dff = w_out.shape[1]
    nc = dff // MLP_CW
    stacked = prev_states is not None
    tm = 1024
    tiles = SEQ // tm
    chunk = lambda i: jnp.minimum(i, nc - 1)
    tile = lambda i: jnp.maximum(i - nc, 0)
    half_spec = lambda blk: _const_spec((SMALL_BLOCK, D_MODEL), (blk, 0))
    lead = (2,) if stacked else ()
    zero = (0,) if stacked else ()
    prev_specs = [
        pl.BlockSpec((DEC_BATCH, 2, MLP_CW), lambda i: (0, 0, chunk(i))),
        pl.BlockSpec((1, 2, dff), lambda i: (tile(i) // tiles, 0, 0)),
    ] if stacked else []
    return pl.pallas_call(
        functools.partial(_mlp_layer_kernel, tm=tm, dff=dff, kind=kind, final_norm=final_norm,
                          stacked=stacked, g_layer=g_layer, layer=layer),
        grid=(nc + BATCH * tiles,),
        in_specs=[
            half_spec(x_meta[1]),
            half_spec(x_samp[1]),
            pl.BlockSpec((tm, D_MODEL), lambda i: (tile(i), 0)),
            _const_spec(g.shape),
            *[pl.BlockSpec((None, D_MODEL, MLP_CW), functools.partial(
                lambda i, p: (layer, 0, p * nc + chunk(i)), p=p)) for p in range(npart)],
            _const_spec(conv_w.shape),
            _const_spec(conv_b.shape),
            pl.BlockSpec((None, MLP_CW, D_MODEL), lambda i: (layer, chunk(i), 0)),
            pl.BlockSpec((None, DEC_BATCH, 2, MLP_CW), lambda i: (layer, 0, 0, chunk(i))),
            _const_spec((None, 1, D_MODEL), (0, 0, 0)),
            *prev_specs,
        ],
        out_specs=[
            pl.BlockSpec((SMALL_ROWS, D_MODEL), lambda i: (0, 0)),
            pl.BlockSpec(lead + (DEC_BATCH, 2, MLP_CW), lambda i: zero + (0, 0, chunk(i))),
            pl.BlockSpec((tm, D_MODEL), lambda i: (tile(i), 0)),
            pl.BlockSpec(lead + (1, 2, dff), lambda i: zero + (tile(i) // tiles, 0, 0)),
        ],
        out_shape=[
            jax.ShapeDtypeStruct((SMALL_ROWS, D_MODEL), F32),
            jax.ShapeDtypeStruct(lead + (DEC_BATCH, 2, dff), F32),
            jax.ShapeDtypeStruct(xb.shape, F32),
            jax.ShapeDtypeStruct(lead + (BATCH, 2, dff), F32),
        ],
        scratch_shapes=[
            *[pltpu.VMEM((D_MODEL, dff), BF16) for _ in range(npart)],
            pltpu.VMEM((dff, D_MODEL), BF16),
            pltpu.VMEM((SMALL_ROWS, D_MODEL), BF16),
            pltpu.VMEM((SMALL_ROWS, D_MODEL), F32),
            pltpu.VMEM((SUBLANES, dff), F32),
            pltpu.VMEM((SUBLANES, dff), F32),
            pltpu.VMEM((SMALL_BLOCK + SUBLANES, MLP_CW), F32),
            pltpu.VMEM((tm, dff), BF16),
        ],
        compiler_params=_params(1),
        name="mlp_layer_" + kind,
    )(x_meta[0], x_samp[0], xb, g, *([w_in] * npart), conv_w, conv_b, w_out, state, g_final,
      *(prev_states or ()))


def _rope_angles(pos):
    half = RET_DK // 2
    inv = 1.0 / (ROPE_BASE ** jnp.linspace(0.0, 1.0, half, dtype=F32))
    return jnp.asarray(pos, F32)[:, None] * jnp.repeat(inv, 2)[None, :]


def _rope_sign():
    return jnp.where(jnp.arange(RET_DK) % 2 == 0, -1.0, 1.0).astype(F32)[None, :]


def _rope_tables(pos):
    ang = _rope_angles(pos)
    return jnp.cos(ang), jnp.sin(ang) * _rope_sign()


def _rope_tables_split(first, count):
    coarse = _rope_angles(first + ROT_LO * np.arange(count // ROT_LO))
    fine = _rope_angles(np.arange(ROT_LO))
    sign = _rope_sign()
    cf, sf = jnp.cos(fine), jnp.sin(fine)
    return (jnp.stack([jnp.cos(coarse), jnp.sin(coarse)]),
            jnp.stack([cf, sf, sign * cf, sign * sf]))


def kernel(x_prompt, x_sample, state_ret, state_conv, state_ffn, meta_tokens, norm_mix, norm_ffn,
           norm_final, w_ret_in, w_ret_out, w_sc_in, w_sc_conv, w_sc_out, w_ffn_in, w_ffn_conv,
           b_ffn_conv, w_ffn_out):
    nb = SMALL_BLOCK
    xb = x_prompt.reshape(BATCH * SEQ, D_MODEL)
    g_final = norm_final.reshape(1, 1, D_MODEL)

    pos_small = np.concatenate([np.arange(nb), np.full(nb, PAST_LEN)])
    tabs_big = _rope_tables_split(N_META, SEQ)
    tabs_small = _rope_tables(pos_small)

    proj_s, w_ret_in_b, qkt_s = _ret_inproj_small(meta_tokens, x_sample, norm_mix, w_ret_in,
                                                  tabs_small)
    xs_meta, s_meta, w_ret_out_b = _ret_meta(proj_s, meta_tokens, w_ret_out)
    xb, ret_p, o_s, ret_s = _ret_layer(xb, norm_mix, w_ret_in_b, tabs_big, s_meta, w_ret_out_b,
                                       proj_s, qkt_s, state_ret, chunk=256)
    xs_samp = _gn_outproj(o_s, proj_s, x_sample, w_ret_out_b)

    xs, ffn_s0, xb, ffn_p0 = _mlp_layer(
        (xs_meta, 0), (xs_samp, 0), xb, norm_ffn, 0, w_ffn_in, w_ffn_conv, b_ffn_conv, w_ffn_out,
        state_ffn, 0, g_final, kind="ffn", final_norm=False)

    zero_b = np.zeros((1, D_MODEL), np.float32)
    xs, conv_s, xb, conv_p = _mlp_layer(
        (xs, 0), (xs, 1), xb, norm_mix, 1, w_sc_in, w_sc_conv, zero_b, w_sc_out,
        state_conv, 0, g_final, kind="sc", final_norm=False)

    xs, ffn_s, xb, ffn_p = _mlp_layer(
        (xs, 0), (xs, 1), xb, norm_ffn, 1, w_ffn_in, w_ffn_conv, b_ffn_conv, w_ffn_out,
        state_ffn, 1, g_final, kind="ffn", final_norm=True, prev_states=(ffn_s0, ffn_p0))

    y_prompt = xb.reshape(BATCH, SEQ, D_MODEL)
    y_sample = xs[nb:].reshape(DEC_BATCH, 1, D_MODEL)
    return (y_prompt, y_sample, ret_p[None], ret_s, conv_p[None], conv_s[None], ffn_p, ffn_s)
```
